```python
import jax
import jax.numpy as jnp
from jax import lax
import numpy as np

D_MODEL = 1024
BATCH = 4
SEQ = 4096
DEPTH = 2

CHUNK = 64
MEM_LEN = 256
EPS = 1e-6

SB_HEAD_DIM = 64
SB_WIDTH = D_MODEL // 2
SB_HEADS = SB_WIDTH // SB_HEAD_DIM
SB_BLOCK = 128

LRU_WIDTH = D_MODEL // 4
LRU_BLOCKS = 4
LRU_BLOCK_DIM = LRU_WIDTH // LRU_BLOCKS
LRU_CONV = 4
LRU_C = 8.0
LRU_A_MIN = 0.9
LRU_A_MAX = 0.999

SG_WIDTH = D_MODEL // 4
SG_GROUPS = 4
SG_GROUP_DIM = SG_WIDTH // SG_GROUPS
SG_CHUNK = 128

MIX_WIDTH = SB_WIDTH + LRU_WIDTH + SG_WIDTH
IN_WIDTH = 3 * SB_WIDTH + 2 * LRU_WIDTH + 2 * SG_WIDTH

XA_HEADS = 4
XA_HEAD_DIM = D_MODEL // XA_HEADS

D_FF = ((8 * D_MODEL // 3 + 127) // 128) * 128
FFN_CONV = 3

kernel_name = 'hybrid_sb_rglru_sgmlp_block'


def _rms(x):
    xf = x.astype(jnp.float32)
    return xf * lax.rsqrt(jnp.mean(xf * xf, axis=-1, keepdims=True) + EPS)


def rmsnorm(x, g):
    return (_rms(x) * g.astype(jnp.float32)).astype(x.dtype)


def causal_dwconv(x, w, b):
    k, c = w.shape
    y = lax.conv_general_dilated(x, w[:, None, :].astype(x.dtype), window_strides=(1,), padding=[(k - 1, 0)], dimension_numbers=('NWC', 'WIO', 'NWC'), feature_group_count=c)
    return y + b.astype(x.dtype)


def stick_breaking_attention(q, k, v):
    seq = q.shape[2]
    scale = q.shape[-1] ** -0.5
    outs = []
    for blk in range(seq // SB_BLOCK):
        q0 = blk * SB_BLOCK
        q1 = q0 + SB_BLOCK
        z = jnp.einsum('bhqd,bhkd->bhqk', q[:, :, q0:q1], k[:, :, :q1]).astype(jnp.float32) * scale
        t_idx = q0 + jnp.arange(SB_BLOCK)[:, None]
        s_idx = jnp.arange(q1)[None, :]
        mask = s_idx < t_idx
        log_stay = jnp.where(mask, jax.nn.log_sigmoid(-z), 0.0)
        tail = lax.cumsum(log_stay, axis=3, reverse=True) - log_stay
        w = jnp.where(mask, jnp.exp(jax.nn.log_sigmoid(z) + tail), 0.0)
        outs.append(jnp.einsum('bhqk,bhkd->bhqd', w.astype(v.dtype), v[:, :, :q1]))
    return jnp.concatenate(outs, axis=2)


def rg_lru(x, w_a, b_a, w_x, b_x, lam):
    bsz, seq, c = x.shape
    xf = x.astype(jnp.float32)
    xb = xf.reshape(bsz, seq, LRU_BLOCKS, LRU_BLOCK_DIM)
    r = jax.nn.sigmoid(jnp.einsum('bsnc,ncd->bsnd', xb, w_a.astype(jnp.float32)).reshape(bsz, seq, c) + b_a.astype(jnp.float32))
    i = jax.nn.sigmoid(jnp.einsum('bsnc,ncd->bsnd', xb, w_x.astype(jnp.float32)).reshape(bsz, seq, c) + b_x.astype(jnp.float32))
    log_a = -LRU_C * r * jax.nn.softplus(-lam.astype(jnp.float32))
    a = jnp.exp(log_a)
    u = jnp.sqrt(-jnp.expm1(2.0 * log_a)) * (i * xf)

    def combine(left, right):
        a_l, h_l = left
        a_r, h_r = right
        return a_l * a_r, a_r * h_l + h_r

    _, h = lax.associative_scan(combine, (a, u), axis=1)
    return h.astype(x.dtype)


def spatial_gating(u, v, g_v, w_s, b_s):
    bsz, seq, c = v.shape
    v = rmsnorm(v, g_v)
    vc = v.reshape(bsz, seq // SG_CHUNK, SG_CHUNK, SG_GROUPS, SG_GROUP_DIM)
    pos = jnp.arange(SG_CHUNK)
    mask = (pos[None, :] // CHUNK) <= (pos[:, None] // CHUNK)
    w = jnp.where(mask[None], w_s, 0.0).astype(v.dtype)
    mixed = jnp.einsum('gij,bnjgc->bnigc', w, vc) + jnp.transpose(b_s)[None, None, :, :, None].astype(v.dtype)
    return u * mixed.reshape(bsz, seq, c)


def hybrid_mixer(h, w_in, lru_conv_w, lru_conv_b, lru_w_a, lru_b_a, lru_w_x, lru_b_x, lru_lambda, sg_norm, sg_w, sg_b, mix_norm, w_out):
    bsz, seq, _ = h.shape
    p = h @ w_in
    cuts = np.cumsum([SB_WIDTH, SB_WIDTH, SB_WIDTH, LRU_WIDTH, LRU_WIDTH, SG_WIDTH]).tolist()
    q, k, v, xr, yg, su, sv = jnp.split(p, cuts, axis=-1)

    def heads(t):
        return t.reshape(bsz, seq, SB_HEADS, SB_HEAD_DIM).transpose(0, 2, 1, 3)

    a_out = stick_breaking_attention(heads(q), heads(k), heads(v)).transpose(0, 2, 1, 3).reshape(bsz, seq, SB_WIDTH)
    b_out = rg_lru(causal_dwconv(xr, lru_conv_w, lru_conv_b), lru_w_a, lru_b_a, lru_w_x, lru_b_x, lru_lambda) * jax.nn.gelu(yg)
    c_out = spatial_gating(jax.nn.gelu(su), jax.nn.gelu(sv), sg_norm, sg_w, sg_b)
    mixed = jnp.concatenate([_rms(a_out), _rms(b_out), _rms(c_out)], axis=-1) * mix_norm.astype(jnp.float32)
    return mixed.astype(h.dtype) @ w_out


def memory_cross_attention(h, mem_n, w_q, w_kv, w_o):
    bsz, seq, _ = h.shape
    q = (h @ w_q).reshape(bsz, seq, XA_HEADS, XA_HEAD_DIM)
    k, v = jnp.split(mem_n @ w_kv, 2, axis=-1)
    k = k.reshape(bsz, -1, XA_HEADS, XA_HEAD_DIM)
    v = v.reshape(bsz, -1, XA_HEADS, XA_HEAD_DIM)
    s = jnp.einsum('bshd,bmhd->bhsm', q, k).astype(jnp.float32) * (XA_HEAD_DIM ** -0.5)
    pr = jax.nn.softmax(s, axis=-1).astype(v.dtype)
    o = jnp.einsum('bhsm,bmhd->bshd', pr, v).reshape(bsz, seq, D_MODEL)
    return o @ w_o


def conv_gated_ffn(h, w_up, conv_w, conv_b, w_down):
    z = causal_dwconv(h @ w_up, conv_w, conv_b)
    g, up = jnp.split(z, 2, axis=-1)
    return (jax.nn.gelu(g) * up) @ w_down


def setup_inputs(seed: int = 0) -> dict:
    key = jax.random.key(seed)
    ks = jax.random.split(key, 32)
    L = DEPTH

    def normal(i, shape, scale):
        return scale * jax.random.normal(ks[i], shape, jnp.float32)

    def gain(i, n):
        return 1.0 + 0.05 * jax.random.normal(ks[i], (L, n), jnp.float32)

    a0 = jax.random.uniform(ks[10], (L, LRU_WIDTH), jnp.float32, LRU_A_MIN, LRU_A_MAX)
    s = a0 ** (1.0 / LRU_C)
    lam = jnp.log(s) - jnp.log1p(-s)
    return {
        'x': normal(0, (BATCH, SEQ, D_MODEL), 1.0),
        'mem': normal(1, (BATCH, MEM_LEN, D_MODEL), 1.0),
        'norm_mix_pre': gain(2, D_MODEL),
        'w_in': normal(3, (L, D_MODEL, IN_WIDTH), D_MODEL ** -0.5),
        'lru_conv_w': normal(4, (L, LRU_CONV, LRU_WIDTH), LRU_CONV ** -0.5),
        'lru_conv_b': normal(5, (L, LRU_WIDTH), 0.1),
        'lru_w_a': normal(6, (L, LRU_BLOCKS, LRU_BLOCK_DIM, LRU_BLOCK_DIM), LRU_BLOCK_DIM ** -0.5),
        'lru_b_a': normal(7, (L, LRU_WIDTH), 0.1),
        'lru_w_x': normal(8, (L, LRU_BLOCKS, LRU_BLOCK_DIM, LRU_BLOCK_DIM), LRU_BLOCK_DIM ** -0.5),
        'lru_b_x': normal(9, (L, LRU_WIDTH), 0.1),
        'lru_lambda': lam,
        'sg_norm': gain(11, SG_WIDTH),
        'sg_w': normal(12, (L, SG_GROUPS, SG_CHUNK, SG_CHUNK), 0.5 * SG_CHUNK ** -0.5),
        'sg_b': 1.0 + normal(13, (L, SG_GROUPS, SG_CHUNK), 0.1),
        'mix_norm': gain(14, MIX_WIDTH),
        'w_out': normal(15, (L, MIX_WIDTH, D_MODEL), MIX_WIDTH ** -0.5),
        'norm_mix_post': gain(16, D_MODEL),
        'norm_xa_pre': gain(17, D_MODEL),
        'norm_mem': gain(18, D_MODEL),
        'xa_w_q': normal(19, (L, D_MODEL, D_MODEL), D_MODEL ** -0.5),
        'xa_w_kv': normal(20, (L, D_MODEL, 2 * D_MODEL), D_MODEL ** -0.5),
        'xa_w_o': normal(21, (L, D_MODEL, D_MODEL), D_MODEL ** -0.5),
        'norm_xa_post': gain(22, D_MODEL),
        'norm_ffn_pre': gain(23, D_MODEL),
        'ffn_w_up': normal(24, (L, D_MODEL, 2 * D_FF), D_MODEL ** -0.5),
        'ffn_conv_w': normal(25, (L, FFN_CONV, 2 * D_FF), FFN_CONV ** -0.5),
        'ffn_conv_b': normal(26, (L, 2 * D_FF), 0.1),
        'ffn_w_down': normal(27, (L, D_FF, D_MODEL), D_FF ** -0.5),
        'norm_ffn_post': gain(28, D_MODEL),
    }


def reference(x, mem, norm_mix_pre, w_in, lru_conv_w, lru_conv_b, lru_w_a, lru_b_a, lru_w_x, lru_b_x, lru_lambda, sg_norm, sg_w, sg_b, mix_norm, w_out, norm_mix_post, norm_xa_pre, norm_mem, xa_w_q, xa_w_kv, xa_w_o, norm_xa_post, norm_ffn_pre, ffn_w_up, ffn_conv_w, ffn_conv_b, ffn_w_down, norm_ffn_post):
    for l in range(DEPTH):
        h = rmsnorm(x, norm_mix_pre[l])
        y = hybrid_mixer(h, w_in[l], lru_conv_w[l], lru_conv_b[l], lru_w_a[l], lru_b_a[l], lru_w_x[l], lru_b_x[l], lru_lambda[l], sg_norm[l], sg_w[l], sg_b[l], mix_norm[l], w_out[l])
        x = x + rmsnorm(y, norm_mix_post[l])
        h = rmsnorm(x, norm_xa_pre[l])
        y = memory_cross_attention(h, rmsnorm(mem, norm_mem[l]), xa_w_q[l], xa_w_kv[l], xa_w_o[l])
        x = x + rmsnorm(y, norm_xa_post[l])
        h = rmsnorm(x, norm_ffn_pre[l])
        y = conv_gated_ffn(h, ffn_w_up[l], ffn_conv_w[l], ffn_conv_b[l], ffn_w_down[l])
        x = x + rmsnorm(y, norm_ffn_post[l])
    return x
```

```python
import functools

import jax
import jax.numpy as jnp
from jax import lax
from jax.experimental import pallas as pl
from jax.experimental.pallas import tpu as pltpu

F32 = jnp.float32
BF16 = jnp.bfloat16

D_MODEL = 1024
CHUNK = 64
EPS = 1e-6
SB_HEAD_DIM = 64
SB_WIDTH = 512
LRU_WIDTH = 256
LRU_BLOCKS = 4
LRU_CONV = 4
LRU_C = 8.0
SG_WIDTH = 256
SG_GROUPS = 4
SG_GROUP_DIM = 64
SG_CHUNK = 128
IN_WIDTH = 2560
QKV_WIDTH = 3 * SB_WIDTH
REST_WIDTH = IN_WIDTH - QKV_WIDTH
XA_HEADS = 4
XA_HEAD_DIM = 256
D_FF = 2816
FFN_CONV = 3

SUBLANES = 8
LANES = 128
VMEM_LIMIT_BYTES = 56 * 1024 * 1024

ROW_TILE = 512
SB_BLOCK = 128
MIX_TILE = 256
FF_CHUNK = 256
SB_DEAD_LOG = -110.0


def _params(*semantics):
    return pltpu.CompilerParams(dimension_semantics=semantics, vmem_limit_bytes=VMEM_LIMIT_BYTES)


def _const_spec(shape):
    zeros = (0,) * len(shape)
    return pl.BlockSpec(shape, lambda *_: zeros, pipeline_mode=pl.Buffered(1))


def _rms(x):
    return x * lax.rsqrt(jnp.mean(x * x, axis=-1, keepdims=True) + EPS)


def _gelu(x):
    return 0.5 * x * (1.0 + jnp.tanh(0.7978845608028654 * (x + 0.044715 * (x * x * x))))


def _sigmoid(x):
    return 1.0 / (1.0 + jnp.exp(-x))


def _in_proj_kernel(x_ref, g_ref, w_ref, qkv_ref, rest_ref):
    h = (_rms(x_ref[...]) * g_ref[...]).astype(BF16)
    step = 512
    for c in range(IN_WIDTH // step):
        p = jnp.dot(h, w_ref[:, c * step:(c + 1) * step], preferred_element_type=F32)
        if c == 0:
            qkv_ref[:, 0:step] = (p * (SB_HEAD_DIM ** -0.5)).astype(BF16)
        elif c < QKV_WIDTH // step:
            qkv_ref[:, c * step:(c + 1) * step] = p.astype(BF16)
        else:
            rest_ref[:, c * step - QKV_WIDTH:(c + 1) * step - QKV_WIDTH] = p


def _in_proj(x2, g, w_bf16):
    t = x2.shape[0]
    return pl.pallas_call(
        _in_proj_kernel,
        grid=(t // ROW_TILE,),
        in_specs=[
            pl.BlockSpec((ROW_TILE, D_MODEL), lambda i: (i, 0)),
            _const_spec((1, D_MODEL)),
            _const_spec((D_MODEL, IN_WIDTH)),
        ],
        out_specs=[
            pl.BlockSpec((ROW_TILE, QKV_WIDTH), lambda i: (i, 0)),
            pl.BlockSpec((ROW_TILE, REST_WIDTH), lambda i: (i, 0)),
        ],
        out_shape=[
            jax.ShapeDtypeStruct((t, QKV_WIDTH), BF16),
            jax.ShapeDtypeStruct((t, REST_WIDTH), F32),
        ],
        compiler_params=_params("parallel"),
        name="in_proj",
    )(x2, g, w_bf16)


def _sb_attn_kernel(q_ref, k_ref, v_ref, o_ref):
    i = pl.program_id(2)
    blk = SB_BLOCK
    row = lax.broadcasted_iota(jnp.int32, (blk, blk), 0)
    col = lax.broadcasted_iota(jnp.int32, (blk, blk), 1)
    causal = col < row
    suffix_ones = (row >= col).astype(BF16)
    lane = lax.broadcasted_iota(jnp.int32, (blk, LANES), 1)
    q = q_ref[...]

    def tile(qh, j, carry, acc, diag):
        start = pl.multiple_of(j * blk, blk)
        kj = k_ref[pl.ds(start, blk), :]
        vj = v_ref[pl.ds(start, blk), :]
        s = lax.dot_general(qh, kj, (((1,), (1,)), ((), ())), preferred_element_type=F32)
        ls = -(jnp.maximum(s, 0.0) + jnp.log(1.0 + jnp.exp(-jnp.abs(s))))
        if diag:
            ls = jnp.where(causal, ls, 0.0)
        hi = ls.astype(BF16)
        lo = (ls - hi.astype(F32)).astype(BF16)
        incl = (jnp.dot(hi, suffix_ones, preferred_element_type=F32)
                + jnp.dot(lo, suffix_ones, preferred_element_type=F32))
        w = jnp.exp(s + incl + carry)
        if diag:
            w = jnp.where(causal, w, 0.0)
        acc = acc + jnp.dot(w.astype(BF16), vj, preferred_element_type=F32)
        carry = carry + incl[:, 0:1]
        return carry, acc

    accs = []
    for h in range(2):
        head_lanes = (lane < SB_HEAD_DIM) if h == 0 else (lane >= SB_HEAD_DIM)
        qh = jnp.where(head_lanes, q, jnp.zeros_like(q))
        carry0 = jnp.zeros((blk, 1), F32)
        acc0 = jnp.zeros((blk, LANES), F32)
        carry, acc = tile(qh, i, carry0, acc0, True)

        def cond(state):
            j, carry, _ = state
            return jnp.logical_and(j >= 0, jnp.max(carry) > SB_DEAD_LOG)

        def body(state, qh=qh):
            j, carry, acc = state
            carry, acc = tile(qh, j, carry, acc, False)
            return j - 1, carry, acc

        _, _, acc = lax.while_loop(cond, body, (i - 1, carry, acc))
        accs.append(acc)
    o_ref[...] = jnp.where(lane < SB_HEAD_DIM, accs[0], accs[1])


def _sb_attention(qkv3):
    b, s, _ = qkv3.shape
    pairs = SB_WIDTH // LANES
    return pl.pallas_call(
        _sb_attn_kernel,
        grid=(b, pairs, s // SB_BLOCK),
        in_specs=[
            pl.BlockSpec((None, SB_BLOCK, LANES), lambda bi, p, i: (bi, i, p)),
            pl.BlockSpec((None, s, LANES), lambda bi, p, i: (bi, 0, pairs + p)),
            pl.BlockSpec((None, s, LANES), lambda bi, p, i: (bi, 0, 2 * pairs + p)),
        ],
        out_specs=pl.BlockSpec((None, SB_BLOCK, LANES), lambda bi, p, i: (bi, i, p)),
        out_shape=jax.ShapeDtypeStruct((b, s, SB_WIDTH), F32),
        compiler_params=_params("parallel", "parallel", "parallel"),
        name="sb_attn",
    )(qkv3, qkv3, qkv3)


def _mix_kernel(rest_ref, halo_ref, cw_ref, cb_ref, wa_ref, ba_ref, wx_ref, bx_ref, lam_ref,
                sgn_ref, sgw_ref, sgb_ref, b_ref, c_ref, xpad_ref, h_ref):
    i = pl.program_id(1)
    tm = MIX_TILE
    w = LRU_WIDTH

    @pl.when(i == 0)
    def _():
        h_ref[...] = jnp.zeros_like(h_ref)

    xr = rest_ref[:, 0:w]
    xpad_ref[0:SUBLANES, :] = jnp.where(i == 0, 0.0, halo_ref[...])
    xpad_ref[SUBLANES:SUBLANES + tm, :] = xr
    xc = cb_ref[...] + cw_ref[LRU_CONV - 1:LRU_CONV, :] * xr
    for k in range(LRU_CONV - 1):
        off = SUBLANES - (LRU_CONV - 1) + k
        xc = xc + cw_ref[k:k + 1, :] * xpad_ref[off:off + tm, :]

    xcb = xc.astype(BF16)
    r = _sigmoid(jnp.dot(xcb, wa_ref[...], preferred_element_type=F32) + ba_ref[...])
    gate_i = _sigmoid(jnp.dot(xcb, wx_ref[...], preferred_element_type=F32) + bx_ref[...])
    lam = lam_ref[...]
    softplus_neg_lam = jnp.maximum(-lam, 0.0) + jnp.log1p(jnp.exp(-jnp.abs(lam)))
    log_a = (-LRU_C) * r * softplus_neg_lam
    a = jnp.exp(log_a)
    u = jnp.sqrt(-jnp.tanh(log_a) * (a * a + 1.0)) * (gate_i * xc)

    row = lax.broadcasted_iota(jnp.int32, (tm, w), 0)
    d = 1
    while d < tm:
        valid = row >= d
        a_sh = jnp.where(valid, pltpu.roll(a, d, 0), 1.0)
        u_sh = jnp.where(valid, pltpu.roll(u, d, 0), 0.0)
        u = a * u_sh + u
        a = a * a_sh
        d *= 2
    hcur = u + a * h_ref[0:1, :]
    h_ref[...] = jnp.broadcast_to(hcur[tm - 1:tm, :], h_ref.shape)
    b_ref[...] = hcur * _gelu(rest_ref[:, w:2 * w])

    su = _gelu(rest_ref[:, 2 * w:3 * w])
    sv = _gelu(rest_ref[:, 3 * w:4 * w])
    vn = (_rms(sv) * sgn_ref[...]).astype(BF16)
    pi = lax.broadcasted_iota(jnp.int32, (SG_CHUNK, SG_CHUNK), 0)
    pj = lax.broadcasted_iota(jnp.int32, (SG_CHUNK, SG_CHUNK), 1)
    chunk_causal = (pj // CHUNK) <= (pi // CHUNK)
    lane = lax.broadcasted_iota(jnp.int32, (SG_CHUNK, SG_WIDTH), 1)
    wm = [jnp.where(chunk_causal, sgw_ref[g], 0.0).astype(BF16) for g in range(SG_GROUPS)]
    for n in range(tm // SG_CHUNK):
        vchunk = vn[n * SG_CHUNK:(n + 1) * SG_CHUNK, :]
        mixed = sgb_ref[...]
        for g in range(SG_GROUPS):
            mg = jnp.dot(wm[g], vchunk, preferred_element_type=F32)
            in_group = (lane // SG_GROUP_DIM) == g
            mixed = mixed + jnp.where(in_group, mg, 0.0)
        c_ref[n * SG_CHUNK:(n + 1) * SG_CHUNK, :] = su[n * SG_CHUNK:(n + 1) * SG_CHUNK, :] * mixed


def _mix_branches(rest3, cw, cb, wa_bd, ba, wx_bd, bx, lam, sgn, sgw, sgb_full):
    b, s, _ = rest3.shape
    tm = MIX_TILE
    w = LRU_WIDTH
    halo_blocks = tm // SUBLANES
    return pl.pallas_call(
        _mix_kernel,
        grid=(b, s // tm),
        in_specs=[
            pl.BlockSpec((None, tm, REST_WIDTH), lambda bi, i: (bi, i, 0)),
            pl.BlockSpec((None, SUBLANES, w), lambda bi, i: (bi, jnp.maximum(i * halo_blocks - 1, 0), 0)),
            _const_spec((LRU_CONV, w)),
            _const_spec((1, w)),
            _const_spec((w, w)),
            _const_spec((1, w)),
            _const_spec((w, w)),
            _const_spec((1, w)),
            _const_spec((1, w)),
            _const_spec((1, SG_WIDTH)),
            _const_spec((SG_GROUPS, SG_CHUNK, SG_CHUNK)),
            _const_spec((SG_CHUNK, SG_WIDTH)),
        ],
        out_specs=[
            pl.BlockSpec((None, tm, w), lambda bi, i: (bi, i, 0)),
            pl.BlockSpec((None, tm, SG_WIDTH), lambda bi, i: (bi, i, 0)),
        ],
        out_shape=[
            jax.ShapeDtypeStruct((b, s, w), F32),
            jax.ShapeDtypeStruct((b, s, SG_WIDTH), F32),
        ],
        scratch_shapes=[
            pltpu.VMEM((tm + SUBLANES, w), F32),
            pltpu.VMEM((SUBLANES, w), F32),
        ],
        compiler_params=_params("parallel", "arbitrary"),
        name="mix_branches",
    )(rest3, rest3, cw, cb, wa_bd, ba, wx_bd, bx, lam, sgn, sgw, sgb_full)


def _mix_out_kernel(x_ref, a_ref, b_ref, c_ref, mn_ref, w_ref, g_ref, o_ref):
    mixed = jnp.concatenate([_rms(a_ref[...]), _rms(b_ref[...]), _rms(c_ref[...])], axis=-1)
    mixed = (mixed * mn_ref[...]).astype(BF16)
    y = jnp.dot(mixed, w_ref[...], preferred_element_type=F32)
    o_ref[...] = x_ref[...] + _rms(y) * g_ref[...]


def _mix_out(x2, a2, b2, c2, mn, w_bf16, g):
    t = x2.shape[0]
    tm = ROW_TILE
    return pl.pallas_call(
        _mix_out_kernel,
        grid=(t // tm,),
        in_specs=[
            pl.BlockSpec((tm, D_MODEL), lambda i: (i, 0)),
            pl.BlockSpec((tm, SB_WIDTH), lambda i: (i, 0)),
            pl.BlockSpec((tm, LRU_WIDTH), lambda i: (i, 0)),
            pl.BlockSpec((tm, SG_WIDTH), lambda i: (i, 0)),
            _const_spec((1, D_MODEL)),
            _const_spec((D_MODEL, D_MODEL)),
            _const_spec((1, D_MODEL)),
        ],
        out_specs=pl.BlockSpec((tm, D_MODEL), lambda i: (i, 0)),
        out_shape=jax.ShapeDtypeStruct((t, D_MODEL), F32),
        compiler_params=_params("parallel"),
        name="mix_out",
    )(x2, a2, b2, c2, mn, w_bf16, g)


def _mem_kv_kernel(mem_ref, g_ref, w_ref, kv_ref):
    mn = (_rms(mem_ref[...]) * g_ref[...]).astype(BF16)
    kv_ref[...] = jnp.dot(mn, w_ref[...], preferred_element_type=F32).astype(BF16)


def _mem_kv(mem, g, w_bf16):
    b, m, _ = mem.shape
    return pl.pallas_call(
        _mem_kv_kernel,
        grid=(b,),
        in_specs=[
            pl.BlockSpec((None, m, D_MODEL), lambda bi: (bi, 0, 0)),
            _const_spec((1, D_MODEL)),
            _const_spec((D_MODEL, 2 * D_MODEL)),
        ],
        out_specs=pl.BlockSpec((None, m, 2 * D_MODEL), lambda bi: (bi, 0, 0)),
        out_shape=jax.ShapeDtypeStruct((b, m, 2 * D_MODEL), BF16),
        compiler_params=_params("parallel"),
        name="mem_kv",
    )(mem, g, w_bf16)


def _xattn_kernel(x_ref, gpre_ref, wq_ref, kv_ref, wo_ref, gpost_ref, o_ref):
    x = x_ref[...]
    h = (_rms(x) * gpre_ref[...]).astype(BF16)
    q = (jnp.dot(h, wq_ref[...], preferred_element_type=F32) * (XA_HEAD_DIM ** -0.5)).astype(BF16)
    outs = []
    for hd in range(XA_HEADS):
        lo = hd * XA_HEAD_DIM
        k = kv_ref[:, lo:lo + XA_HEAD_DIM]
        v = kv_ref[:, D_MODEL + lo:D_MODEL + lo + XA_HEAD_DIM]
        s = lax.dot_general(q[:, lo:lo + XA_HEAD_DIM], k, (((1,), (1,)), ((), ())),
                            preferred_element_type=F32)
        e = jnp.exp(s - jnp.max(s, axis=-1, keepdims=True))
        p = (e / jnp.sum(e, axis=-1, keepdims=True)).astype(BF16)
        outs.append(jnp.dot(p, v, preferred_element_type=F32).astype(BF16))
    o = jnp.concatenate(outs, axis=-1)
    y = jnp.dot(o, wo_ref[...], preferred_element_type=F32)
    o_ref[...] = x + _rms(y) * gpost_ref[...]


def _xattn(x3, gpre, wq_bf16, kv, wo_bf16, gpost):
    b, s, _ = x3.shape
    m = kv.shape[1]
    tm = ROW_TILE
    return pl.pallas_call(
        _xattn_kernel,
        grid=(b, s // tm),
        in_specs=[
            pl.BlockSpec((None, tm, D_MODEL), lambda bi, i: (bi, i, 0)),
            _const_spec((1, D_MODEL)),
            _const_spec((D_MODEL, D_MODEL)),
            pl.BlockSpec((None, m, 2 * D_MODEL), lambda bi, i: (bi, 0, 0)),
            _const_spec((D_MODEL, D_MODEL)),
            _const_spec((1, D_MODEL)),
        ],
        out_specs=pl.BlockSpec((None, tm, D_MODEL), lambda bi, i: (bi, i, 0)),
        out_shape=jax.ShapeDtypeStruct((b, s, D_MODEL), F32),
        compiler_params=_params("parallel", "parallel"),
        name="xattn",
    )(x3, gpre, wq_bf16, kv, wo_bf16, gpost)


def _ffn_kernel(x_ref, halo_ref, gpre_ref, wup_ref, cw_ref, cb_ref, wdown_ref, gpost_ref, o_ref,
                z_ref):
    i = pl.program_id(1)
    tm = ROW_TILE
    x = x_ref[...]
    xh = jnp.concatenate([halo_ref[...], x], axis=0)
    h = (_rms(xh) * gpre_ref[...]).astype(BF16)
    rows = lax.broadcasted_iota(jnp.int32, (tm + SUBLANES, 1), 0)
    before_start = jnp.logical_and(i == 0, rows < SUBLANES)

    def conv_chunk(col):
        z = jnp.dot(h, wup_ref[:, col:col + FF_CHUNK], preferred_element_type=F32)
        z_ref[...] = jnp.where(before_start, 0.0, z)
        out = cb_ref[:, col:col + FF_CHUNK]
        for k in range(FFN_CONV):
            off = SUBLANES - (FFN_CONV - 1) + k
            out = out + cw_ref[k:k + 1, col:col + FF_CHUNK] * z_ref[off:off + tm, :]
        return out

    y = jnp.zeros((tm, D_MODEL), F32)
    for c in range(D_FF // FF_CHUNK):
        gate = conv_chunk(c * FF_CHUNK)
        up = conv_chunk(D_FF + c * FF_CHUNK)
        act = (_gelu(gate) * up).astype(BF16)
        y = y + jnp.dot(act, wdown_ref[c * FF_CHUNK:(c + 1) * FF_CHUNK, :], preferred_element_type=F32)
    o_ref[...] = x + _rms(y) * gpost_ref[...]


def _ffn(x3, gpre, wup_bf16, cw, cb, wdown_bf16, gpost):
    b, s, _ = x3.shape
    tm = ROW_TILE
    halo_blocks = tm // SUBLANES
    return pl.pallas_call(
        _ffn_kernel,
        grid=(b, s // tm),
        in_specs=[
            pl.BlockSpec((None, tm, D_MODEL), lambda bi, i: (bi, i, 0)),
            pl.BlockSpec((None, SUBLANES, D_MODEL),
                         lambda bi, i: (bi, jnp.maximum(i * halo_blocks - 1, 0), 0)),
            _const_spec((1, D_MODEL)),
            _const_spec((D_MODEL, 2 * D_FF)),
            _const_spec((FFN_CONV, 2 * D_FF)),
            _const_spec((1, 2 * D_FF)),
            _const_spec((D_FF, D_MODEL)),
            _const_spec((1, D_MODEL)),
        ],
        out_specs=pl.BlockSpec((None, tm, D_MODEL), lambda bi, i: (bi, i, 0)),
        out_shape=jax.ShapeDtypeStruct((b, s, D_MODEL), F32),
        scratch_shapes=[pltpu.VMEM((tm + SUBLANES, FF_CHUNK), F32)],
        compiler_params=_params("parallel", "parallel"),
        name="ffn",
    )(x3, x3, gpre, wup_bf16, cw, cb, wdown_bf16, gpost)


def _block_diag(w):
    n, c, d = w.shape
    eye = jnp.eye(n, dtype=w.dtype)
    return (eye[:, None, :, None] * w[:, :, None, :]).reshape(n * c, n * d)


def kernel(x, mem, norm_mix_pre, w_in, lru_conv_w, lru_conv_b, lru_w_a, lru_b_a, lru_w_x, lru_b_x, lru_lambda, sg_norm, sg_w, sg_b, mix_norm, w_out, norm_mix_post, norm_xa_pre, norm_mem, xa_w_q, xa_w_kv, xa_w_o, norm_xa_post, norm_ffn_pre, ffn_w_up, ffn_conv_w, ffn_conv_b, ffn_w_down, norm_ffn_post):
    b, s, d = x.shape
    depth = w_in.shape[0]
    t = b * s
    row = lambda v: v.reshape(1, -1)

    for l in range(depth):
        qkv, rest = _in_proj(x.reshape(t, d), row(norm_mix_pre[l]), w_in[l].astype(BF16))
        a_out = _sb_attention(qkv.reshape(b, s, QKV_WIDTH))
        sgb_full = jnp.repeat(jnp.transpose(sg_b[l]), SG_GROUP_DIM, axis=1)
        b_out, c_out = _mix_branches(
            rest.reshape(b, s, REST_WIDTH), lru_conv_w[l], row(lru_conv_b[l]),
            _block_diag(lru_w_a[l]).astype(BF16), row(lru_b_a[l]),
            _block_diag(lru_w_x[l]).astype(BF16), row(lru_b_x[l]), row(lru_lambda[l]),
            row(sg_norm[l]), sg_w[l], sgb_full)
        x2 = _mix_out(x.reshape(t, d), a_out.reshape(t, SB_WIDTH), b_out.reshape(t, LRU_WIDTH),
                      c_out.reshape(t, SG_WIDTH), row(mix_norm[l]), w_out[l].astype(BF16),
                      row(norm_mix_post[l]))
        kv = _mem_kv(mem, row(norm_mem[l]), xa_w_kv[l].astype(BF16))
        x3 = _xattn(x2.reshape(b, s, d), row(norm_xa_pre[l]), xa_w_q[l].astype(BF16), kv,
                    xa_w_o[l].astype(BF16), row(norm_xa_post[l]))
        x = _ffn(x3, row(norm_ffn_pre[l]), ffn_w_up[l].astype(BF16), ffn_conv_w[l],
                 row(ffn_conv_b[l]), ffn_w_down[l].astype(BF16), row(norm_ffn_post[l]))
    return x
```

```python
import functools

import jax
import jax.numpy as jnp
from jax import lax
from jax.experimental import pallas as pl
from jax.experimental.pallas import tpu as pltpu

F32 = jnp.float32
BF16 = jnp.bfloat16

D_MODEL = 1024
CHUNK = 64
EPS = 1e-6
SB_HEAD_DIM = 64
SB_WIDTH = 512
LRU_WIDTH = 256
LRU_BLOCKS = 4
LRU_CONV = 4
LRU_C = 8.0
SG_WIDTH = 256
SG_GROUPS = 4
SG_GROUP_DIM = 64
SG_CHUNK = 128
IN_WIDTH = 2560
QKV_WIDTH = 3 * SB_WIDTH
REST_WIDTH = IN_WIDTH - QKV_WIDTH
XA_HEADS = 4
XA_HEAD_DIM = 256
D_FF = 2816
FFN_CONV = 3

SUBLANES = 8
LANES = 128
VMEM_LIMIT_BYTES = 56 * 1024 * 1024

ROW_TILE = 512
SB_BLOCK = 128
MIX_TILE = 256
FF_CHUNK = 256
FFN_Z_SLOTS = 4
SB_DEAD_LOG = -110.0
SB_MASKED_SCORE = -1e30
SB_STATIC_BLOCKS = 3


def _params(*semantics):
    return pltpu.CompilerParams(dimension_semantics=semantics, vmem_limit_bytes=VMEM_LIMIT_BYTES)


def _const_spec(shape):
    zeros = (0,) * len(shape)
    return pl.BlockSpec(shape, lambda *_: zeros, pipeline_mode=pl.Buffered(1))


def _rms(x):
    return x * lax.rsqrt(jnp.mean(x * x, axis=-1, keepdims=True) + EPS)


def _gelu(x):
    return 0.5 * x * (1.0 + jnp.tanh(0.7978845608028654 * (x + 0.044715 * (x * x * x))))


def _sigmoid(x):
    return 1.0 / (1.0 + jnp.exp(-x))


def _in_proj_kernel(x_ref, g_ref, w_ref, qkv_ref, rest_ref):
    h = (_rms(x_ref[...]) * g_ref[...]).astype(BF16)
    step = 512
    for c in range(IN_WIDTH // step):
        p = jnp.dot(h, w_ref[:, c * step:(c + 1) * step], preferred_element_type=F32)
        if c == 0:
            qkv_ref[:, 0:step] = (p * (SB_HEAD_DIM ** -0.5)).astype(BF16)
        elif c < QKV_WIDTH // step:
            qkv_ref[:, c * step:(c + 1) * step] = p.astype(BF16)
        else:
            rest_ref[:, c * step - QKV_WIDTH:(c + 1) * step - QKV_WIDTH] = p


def _in_proj(x2, g, w_bf16):
    t = x2.shape[0]
    return pl.pallas_call(
        _in_proj_kernel,
        grid=(t // ROW_TILE,),
        in_specs=[
            pl.BlockSpec((ROW_TILE, D_MODEL), lambda i: (i, 0)),
            _const_spec((1, D_MODEL)),
            _const_spec((D_MODEL, IN_WIDTH)),
        ],
        out_specs=[
            pl.BlockSpec((ROW_TILE, QKV_WIDTH), lambda i: (i, 0)),
            pl.BlockSpec((ROW_TILE, REST_WIDTH), lambda i: (i, 0)),
        ],
        out_shape=[
            jax.ShapeDtypeStruct((t, QKV_WIDTH), BF16),
            jax.ShapeDtypeStruct((t, REST_WIDTH), F32),
        ],
        compiler_params=_params("parallel"),
        name="in_proj",
    )(x2, g, w_bf16)


def _sb_attn_kernel(q_ref, k_ref, v_ref, o_ref):
    i = pl.program_id(1)
    blk = SB_BLOCK
    pairs = SB_WIDTH // LANES
    row = lax.broadcasted_iota(jnp.int32, (blk, 2 * blk), 0)
    col = lax.broadcasted_iota(jnp.int32, (blk, 2 * blk), 1)
    causal = (col % blk) < row
    trow = lax.broadcasted_iota(jnp.int32, (2 * blk, 2 * blk), 0)
    tcol = lax.broadcasted_iota(jnp.int32, (2 * blk, 2 * blk), 1)
    suffix_ones = jnp.logical_and(trow // blk == tcol // blk, trow >= tcol).astype(BF16)
    head0 = lax.broadcasted_iota(jnp.int32, (blk, LANES), 1) < SB_HEAD_DIM

    def split_heads(t):
        zero = jnp.zeros_like(t)
        return jnp.concatenate([jnp.where(head0, t, zero), jnp.where(head0, zero, t)], axis=0)

    def key_rows(j):
        return pl.ds(pl.multiple_of(j * blk, blk), blk)

    def scores(p, j, allowed):
        cols = slice(p * LANES, (p + 1) * LANES)
        kk = split_heads(k_ref[key_rows(j), cols])
        s = lax.dot_general(q_ref[:, cols], kk, (((1,), (1,)), ((), ())), preferred_element_type=F32)
        if allowed is not None:
            s = jnp.where(allowed, s, SB_MASKED_SCORE)
        return s

    def suffix_sum(s):
        ls = -(jnp.maximum(s, 0.0) + jnp.log(1.0 + jnp.exp(-jnp.abs(s))))
        hi = ls.astype(BF16)
        lo = (ls - hi.astype(F32)).astype(BF16)
        return (jnp.dot(hi, suffix_ones, preferred_element_type=F32)
                + jnp.dot(lo, suffix_ones, preferred_element_type=F32))

    def weighted_values(p, j, s, incl, carry):
        cols = slice(p * LANES, (p + 1) * LANES)
        vv = split_heads(v_ref[key_rows(j), cols])
        c0, c1 = carry
        carry_b = jnp.concatenate([jnp.broadcast_to(c0, (blk, blk)), jnp.broadcast_to(c1, (blk, blk))], axis=1)
        w = jnp.exp(s + incl + carry_b)
        return jnp.dot(w.astype(BF16), vv, preferred_element_type=F32)

    def key_blocks(blocks, carries, accs):
        tiles = [(j, p, allowed) for j, allowed in blocks for p in range(pairs)]
        s_all = [scores(p, j, allowed) for j, p, allowed in tiles]
        incl_all = [suffix_sum(s) for s in s_all]
        carries, accs = list(carries), list(accs)
        for (j, p, _), s, incl in zip(tiles, s_all, incl_all):
            accs[p] = accs[p] + weighted_values(p, j, s, incl, carries[p])
            carries[p] = (carries[p][0] + incl[:, 0:1], carries[p][1] + incl[:, blk:blk + 1])
        return tuple(carries), tuple(accs)

    zero_col = jnp.zeros((blk, 1), F32)
    carries = tuple((zero_col, zero_col) for _ in range(pairs))
    accs = tuple(jnp.zeros((blk, LANES), F32) for _ in range(pairs))
    static_blocks = [(i, causal)] + [(jnp.maximum(i - d, 0), i - d >= 0) for d in range(1, SB_STATIC_BLOCKS)]
    carries, accs = key_blocks(static_blocks, carries, accs)

    def cond(state):
        j, carries, _ = state
        live = functools.reduce(jnp.maximum, [c for pair in carries for c in pair])
        return jnp.logical_and(j >= 0, jnp.max(live) > SB_DEAD_LOG)

    def body(state):
        j, carries, accs = state
        carries, accs = key_blocks([(j, None)], carries, accs)
        return j - 1, carries, accs

    _, _, accs = lax.while_loop(cond, body, (i - SB_STATIC_BLOCKS, carries, accs))
    for p in range(pairs):
        o_ref[:, p * LANES:(p + 1) * LANES] = accs[p]


def _sb_attention(qkv3):
    b, s, _ = qkv3.shape
    return pl.pallas_call(
        _sb_attn_kernel,
        grid=(b, s // SB_BLOCK),
        in_specs=[
            pl.BlockSpec((None, SB_BLOCK, SB_WIDTH), lambda bi, i: (bi, i, 0)),
            pl.BlockSpec((None, s, SB_WIDTH), lambda bi, i: (bi, 0, 1)),
            pl.BlockSpec((None, s, SB_WIDTH), lambda bi, i: (bi, 0, 2)),
        ],
        out_specs=pl.BlockSpec((None, SB_BLOCK, SB_WIDTH), lambda bi, i: (bi, i, 0)),
        out_shape=jax.ShapeDtypeStruct((b, s, SB_WIDTH), F32),
        compiler_params=_params("parallel", "parallel"),
        name="sb_attn",
    )(qkv3, qkv3, qkv3)


def _mix_kernel(rest_ref, halo_ref, cw_ref, cb_ref, wa_ref, ba_ref, wx_ref, bx_ref, lam_ref,
                sgn_ref, sgw_ref, sgb_ref, b_ref, c_ref, xpad_ref, h_ref):
    i = pl.program_id(1)
    tm = MIX_TILE
    w = LRU_WIDTH

    @pl.when(i == 0)
    def _():
        h_ref[...] = jnp.zeros_like(h_ref)

    xr = rest_ref[:, 0:w]
    xpad_ref[0:SUBLANES, :] = jnp.where(i == 0, 0.0, halo_ref[...])
    xpad_ref[SUBLANES:SUBLANES + tm, :] = xr
    xc = cb_ref[...] + cw_ref[LRU_CONV - 1:LRU_CONV, :] * xr
    for k in range(LRU_CONV - 1):
        off = SUBLANES - (LRU_CONV - 1) + k
        xc = xc + cw_ref[k:k + 1, :] * xpad_ref[off:off + tm, :]

    xcb = xc.astype(BF16)
    r = _sigmoid(jnp.dot(xcb, wa_ref[...], preferred_element_type=F32) + ba_ref[...])
    gate_i = _sigmoid(jnp.dot(xcb, wx_ref[...], preferred_element_type=F32) + bx_ref[...])
    lam = lam_ref[...]
    softplus_neg_lam = jnp.maximum(-lam, 0.0) + jnp.log1p(jnp.exp(-jnp.abs(lam)))
    log_a = (-LRU_C) * r * softplus_neg_lam
    a = jnp.exp(log_a)
    u = jnp.sqrt(-jnp.tanh(log_a) * (a * a + 1.0)) * (gate_i * xc)

    row = lax.broadcasted_iota(jnp.int32, (tm, w), 0)
    d = 1
    while d < tm:
        valid = row >= d
        a_sh = jnp.where(valid, pltpu.roll(a, d, 0), 1.0)
        u_sh = jnp.where(valid, pltpu.roll(u, d, 0), 0.0)
        u = a * u_sh + u
        a = a * a_sh
        d *= 2
    hcur = u + a * h_ref[0:1, :]
    h_ref[...] = jnp.broadcast_to(hcur[tm - 1:tm, :], h_ref.shape)
    b_ref[...] = hcur * _gelu(rest_ref[:, w:2 * w])

    su = _gelu(rest_ref[:, 2 * w:3 * w])
    sv = _gelu(rest_ref[:, 3 * w:4 * w])
    vn = (_rms(sv) * sgn_ref[...]).astype(BF16)
    pi = lax.broadcasted_iota(jnp.int32, (SG_CHUNK, SG_CHUNK), 0)
    pj = lax.broadcasted_iota(jnp.int32, (SG_CHUNK, SG_CHUNK), 1)
    chunk_causal = (pj // CHUNK) <= (pi // CHUNK)
    lane = lax.broadcasted_iota(jnp.int32, (SG_CHUNK, SG_WIDTH), 1)
    wm = [jnp.where(chunk_causal, sgw_ref[g], 0.0).astype(BF16) for g in range(SG_GROUPS)]
    for n in range(tm // SG_CHUNK):
        vchunk = vn[n * SG_CHUNK:(n + 1) * SG_CHUNK, :]
        mixed = sgb_ref[...]
        for g in range(SG_GROUPS):
            mg = jnp.dot(wm[g], vchunk, preferred_element_type=F32)
            in_group = (lane // SG_GROUP_DIM) == g
            mixed = mixed + jnp.where(in_group, mg, 0.0)
        c_ref[n * SG_CHUNK:(n + 1) * SG_CHUNK, :] = su[n * SG_CHUNK:(n + 1) * SG_CHUNK, :] * mixed


def _mix_branches(rest3, cw, cb, wa_bd, ba, wx_bd, bx, lam, sgn, sgw, sgb_full):
    b, s, _ = rest3.shape
    tm = MIX_TILE
    w = LRU_WIDTH
    halo_blocks = tm // SUBLANES
    return pl.pallas_call(
        _mix_kernel,
        grid=(b, s // tm),
        in_specs=[
            pl.BlockSpec((None, tm, REST_WIDTH), lambda bi, i: (bi, i, 0)),
            pl.BlockSpec((None, SUBLANES, w), lambda bi, i: (bi, jnp.maximum(i * halo_blocks - 1, 0), 0)),
            _const_spec((LRU_CONV, w)),
            _const_spec((1, w)),
            _const_spec((w, w)),
            _const_spec((1, w)),
            _const_spec((w, w)),
            _const_spec((1, w)),
            _const_spec((1, w)),
            _const_spec((1, SG_WIDTH)),
            _const_spec((SG_GROUPS, SG_CHUNK, SG_CHUNK)),
            _const_spec((SG_CHUNK, SG_WIDTH)),
        ],
        out_specs=[
            pl.BlockSpec((None, tm, w), lambda bi, i: (bi, i, 0)),
            pl.BlockSpec((None, tm, SG_WIDTH), lambda bi, i: (bi, i, 0)),
        ],
        out_shape=[
            jax.ShapeDtypeStruct((b, s, w), F32),
            jax.ShapeDtypeStruct((b, s, SG_WIDTH), F32),
        ],
        scratch_shapes=[
            pltpu.VMEM((tm + SUBLANES, w), F32),
            pltpu.VMEM((SUBLANES, w), F32),
        ],
        compiler_params=_params("parallel", "arbitrary"),
        name="mix_branches",
    )(rest3, rest3, cw, cb, wa_bd, ba, wx_bd, bx, lam, sgn, sgw, sgb_full)


def _mix_out_kernel(x_ref, a_ref, b_ref, c_ref, mn_ref, w_ref, g_ref, o_ref):
    mixed = jnp.concatenate([_rms(a_ref[...]), _rms(b_ref[...]), _rms(c_ref[...])], axis=-1)
    mixed = (mixed * mn_ref[...]).astype(BF16)
    y = jnp.dot(mixed, w_ref[...], preferred_element_type=F32)
    o_ref[...] = x_ref[...] + _rms(y) * g_ref[...]


def _mix_out(x2, a2, b2, c2, mn, w_bf16, g):
    t = x2.shape[0]
    tm = ROW_TILE
    return pl.pallas_call(
        _mix_out_kernel,
        grid=(t // tm,),
        in_specs=[
            pl.BlockSpec((tm, D_MODEL), lambda i: (i, 0)),
            pl.BlockSpec((tm, SB_WIDTH), lambda i: (i, 0)),
            pl.BlockSpec((tm, LRU_WIDTH), lambda i: (i, 0)),
            pl.BlockSpec((tm, SG_WIDTH), lambda i: (i, 0)),
            _const_spec((1, D_MODEL)),
            _const_spec((D_MODEL, D_MODEL)),
            _const_spec((1, D_MODEL)),
        ],
        out_specs=pl.BlockSpec((tm, D_MODEL), lambda i: (i, 0)),
        out_shape=jax.ShapeDtypeStruct((t, D_MODEL), F32),
        compiler_params=_params("parallel"),
        name="mix_out",
    )(x2, a2, b2, c2, mn, w_bf16, g)


def _mem_kv_kernel(mem_ref, g_ref, w_ref, kv_ref):
    mn = (_rms(mem_ref[...]) * g_ref[...]).astype(BF16)
    kv_ref[...] = jnp.dot(mn, w_ref[...], preferred_element_type=F32).astype(BF16)


def _mem_kv(mem, g, w_bf16):
    b, m, _ = mem.shape
    return pl.pallas_call(
        _mem_kv_kernel,
        grid=(b,),
        in_specs=[
            pl.BlockSpec((None, m, D_MODEL), lambda bi: (bi, 0, 0)),
            _const_spec((1, D_MODEL)),
            _const_spec((D_MODEL, 2 * D_MODEL)),
        ],
        out_specs=pl.BlockSpec((None, m, 2 * D_MODEL), lambda bi: (bi, 0, 0)),
        out_shape=jax.ShapeDtypeStruct((b, m, 2 * D_MODEL), BF16),
        compiler_params=_params("parallel"),
        name="mem_kv",
    )(mem, g, w_bf16)


def _xattn_kernel(x_ref, gpre_ref, wq_ref, kv_ref, wo_ref, gpost_ref, o_ref):
    x = x_ref[...]
    h = (_rms(x) * gpre_ref[...]).astype(BF16)
    q = (jnp.dot(h, wq_ref[...], preferred_element_type=F32) * (XA_HEAD_DIM ** -0.5)).astype(BF16)
    outs = []
    for hd in range(XA_HEADS):
        lo = hd * XA_HEAD_DIM
        k = kv_ref[:, lo:lo + XA_HEAD_DIM]
        v = kv_ref[:, D_MODEL + lo:D_MODEL + lo + XA_HEAD_DIM]
        s = lax.dot_general(q[:, lo:lo + XA_HEAD_DIM], k, (((1,), (1,)), ((), ())),
                            preferred_element_type=F32)
        e = jnp.exp(s - jnp.max(s, axis=-1, keepdims=True))
        p = (e / jnp.sum(e, axis=-1, keepdims=True)).astype(BF16)
        outs.append(jnp.dot(p, v, preferred_element_type=F32).astype(BF16))
    o = jnp.concatenate(outs, axis=-1)
    y = jnp.dot(o, wo_ref[...], preferred_element_type=F32)
    o_ref[...] = x + _rms(y) * gpost_ref[...]


def _xattn(x3, gpre, wq_bf16, kv, wo_bf16, gpost):
    b, s, _ = x3.shape
    m = kv.shape[1]
    tm = ROW_TILE
    return pl.pallas_call(
        _xattn_kernel,
        grid=(b, s // tm),
        in_specs=[
            pl.BlockSpec((None, tm, D_MODEL), lambda bi, i: (bi, i, 0)),
            _const_spec((1, D_MODEL)),
            _const_spec((D_MODEL, D_MODEL)),
            pl.BlockSpec((None, m, 2 * D_MODEL), lambda bi, i: (bi, 0, 0)),
            _const_spec((D_MODEL, D_MODEL)),
            _const_spec((1, D_MODEL)),
        ],
        out_specs=pl.BlockSpec((None, tm, D_MODEL), lambda bi, i: (bi, i, 0)),
        out_shape=jax.ShapeDtypeStruct((b, s, D_MODEL), F32),
        compiler_params=_params("parallel", "parallel"),
        name="xattn",
    )(x3, gpre, wq_bf16, kv, wo_bf16, gpost)


def _ffn_kernel(x_ref, halo_ref, gpre_ref, wup_ref, cw_ref, cb_ref, wdown_ref, gpost_ref, o_ref,
                z_ref):
    i = pl.program_id(1)
    tm = ROW_TILE
    x = x_ref[...]
    xh = jnp.concatenate([halo_ref[...], x], axis=0)
    h = (_rms(xh) * gpre_ref[...]).astype(BF16)

    def up_proj(col):
        return jnp.dot(h, wup_ref[:, col:col + FF_CHUNK], preferred_element_type=F32)

    def conv(z, col, slot):
        halves = []
        for hf in range(FF_CHUNK // LANES):
            lanes = slice(hf * LANES, (hf + 1) * LANES)
            wcol = slice(col + hf * LANES, col + (hf + 1) * LANES)
            z_ref[slot, hf, 0:SUBLANES, :] = jnp.where(i == 0, 0.0, z[0:SUBLANES, lanes])
            z_ref[slot, hf, SUBLANES:, :] = z[SUBLANES:, lanes]
            out = cb_ref[:, wcol] + cw_ref[FFN_CONV - 1:FFN_CONV, wcol] * z[SUBLANES:, lanes]
            for k in range(FFN_CONV - 1):
                off = SUBLANES - (FFN_CONV - 1) + k
                out = out + cw_ref[k:k + 1, wcol] * z_ref[slot, hf, off:off + tm, :]
            halves.append(out)
        return jnp.concatenate(halves, axis=-1)

    n_chunks = D_FF // FF_CHUNK
    y = jnp.zeros((tm, D_MODEL), F32)
    z_next = (up_proj(0), up_proj(D_FF))
    for c in range(n_chunks):
        z_gate, z_up = z_next
        if c + 1 < n_chunks:
            z_next = (up_proj((c + 1) * FF_CHUNK), up_proj(D_FF + (c + 1) * FF_CHUNK))
        gate = conv(z_gate, c * FF_CHUNK, (2 * c) % FFN_Z_SLOTS)
        up = conv(z_up, D_FF + c * FF_CHUNK, (2 * c + 1) % FFN_Z_SLOTS)
        act = (_gelu(gate) * up).astype(BF16)
        y = y + jnp.dot(act, wdown_ref[c * FF_CHUNK:(c + 1) * FF_CHUNK, :], preferred_element_type=F32)
    o_ref[...] = x + _rms(y) * gpost_ref[...]


def _ffn(x3, gpre, wup_bf16, cw, cb, wdown_bf16, gpost):
    b, s, _ = x3.shape
    tm = ROW_TILE
    halo_blocks = tm // SUBLANES
    return pl.pallas_call(
        _ffn_kernel,
        grid=(b, s // tm),
        in_specs=[
            pl.BlockSpec((None, tm, D_MODEL), lambda bi, i: (bi, i, 0)),
            pl.BlockSpec((None, SUBLANES, D_MODEL),
                         lambda bi, i: (bi, jnp.maximum(i * halo_blocks - 1, 0), 0)),
            _const_spec((1, D_MODEL)),
            _const_spec((D_MODEL, 2 * D_FF)),
            _const_spec((FFN_CONV, 2 * D_FF)),
            _const_spec((1, 2 * D_FF)),
            _const_spec((D_FF, D_MODEL)),
            _const_spec((1, D_MODEL)),
        ],
        out_specs=pl.BlockSpec((None, tm, D_MODEL), lambda bi, i: (bi, i, 0)),
        out_shape=jax.ShapeDtypeStruct((b, s, D_MODEL), F32),
        scratch_shapes=[pltpu.VMEM((FFN_Z_SLOTS, FF_CHUNK // LANES, tm + SUBLANES, LANES), F32)],
        compiler_params=_params("parallel", "parallel"),
        name="ffn",
    )(x3, x3, gpre, wup_bf16, cw, cb, wdown_bf16, gpost)


def _block_diag(w):
    n, c, d = w.shape
    eye = jnp.eye(n, dtype=w.dtype)
    return (eye[:, None, :, None] * w[:, :, None, :]).reshape(n * c, n * d)


def kernel(x, mem, norm_mix_pre, w_in, lru_conv_w, lru_conv_b, lru_w_a, lru_b_a, lru_w_x, lru_b_x, lru_lambda, sg_norm, sg_w, sg_b, mix_norm, w_out, norm_mix_post, norm_xa_pre, norm_mem, xa_w_q, xa_w_kv, xa_w_o, norm_xa_post, norm_ffn_pre, ffn_w_up, ffn_conv_w, ffn_conv_b, ffn_w_down, norm_ffn_post):
    b, s, d = x.shape
    depth = w_in.shape[0]
    t = b * s
    row = lambda v: v.reshape(1, -1)

    for l in range(depth):
        qkv, rest = _in_proj(x.reshape(t, d), row(norm_mix_pre[l]), w_in[l].astype(BF16))
        a_out = _sb_attention(qkv.reshape(b, s, QKV_WIDTH))
        sgb_full = jnp.repeat(jnp.transpose(sg_b[l]), SG_GROUP_DIM, axis=1)
        b_out, c_out = _mix_branches(
            rest.reshape(b, s, REST_WIDTH), lru_conv_w[l], row(lru_conv_b[l]),
            _block_diag(lru_w_a[l]).astype(BF16), row(lru_b_a[l]),
            _block_diag(lru_w_x[l]).astype(BF16), row(lru_b_x[l]), row(lru_lambda[l]),
            row(sg_norm[l]), sg_w[l], sgb_full)
        x2 = _mix_out(x.reshape(t, d), a_out.reshape(t, SB_WIDTH), b_out.reshape(t, LRU_WIDTH),
                      c_out.reshape(t, SG_WIDTH), row(mix_norm[l]), w_out[l].astype(BF16),
                      row(norm_mix_post[l]))
        kv = _mem_kv(mem, row(norm_mem[l]), xa_w_kv[l].astype(BF16))
        x3 = _xattn(x2.reshape(b, s, d), row(norm_xa_pre[l]), xa_w_q[l].astype(BF16), kv,
                    xa_w_o[l].astype(BF16), row(norm_xa_post[l]))
        x = _ffn(x3, row(norm_ffn_pre[l]), ffn_w_up[l].astype(BF16), ffn_conv_w[l],
                 row(ffn_conv_b[l]), ffn_w_down[l].astype(BF16), row(norm_ffn_post[l]))
    return x
```

```python
import functools

import jax
import jax.numpy as jnp
from jax import lax
from jax.experimental import pallas as pl
from jax.experimental.pallas import tpu as pltpu

F32 = jnp.float32
BF16 = jnp.bfloat16

D_MODEL = 1024
CHUNK = 64
EPS = 1e-6
SB_HEAD_DIM = 64
SB_WIDTH = 512
LRU_WIDTH = 256
LRU_BLOCKS = 4
LRU_CONV = 4
LRU_C = 8.0
SG_WIDTH = 256
SG_GROUPS = 4
SG_GROUP_DIM = 64
SG_CHUNK = 128
IN_WIDTH = 2560
QKV_WIDTH = 3 * SB_WIDTH
REST_WIDTH = IN_WIDTH - QKV_WIDTH
XA_HEADS = 4
XA_HEAD_DIM = 256
D_FF = 2816
FFN_CONV = 3

SUBLANES = 8
LANES = 128
VMEM_LIMIT_BYTES = 56 * 1024 * 1024

ROW_TILE = 512
XA_TILE = 1024
SB_BLOCK = 128
MIX_TILE = 512
FF_CHUNK = 256
FFN_Z_SLOTS = 4
SB_DEAD_LOG = -110.0
LOG2_E = 1.4426950408889634
SB_MASKED_SCORE = -1e30
SB_STATIC_BLOCKS = 3


def _params(*semantics):
    return pltpu.CompilerParams(dimension_semantics=semantics, vmem_limit_bytes=VMEM_LIMIT_BYTES)


def _layer_spec(l, shape):
    index = (l,) + (0,) * len(shape)
    return pl.BlockSpec((None,) + tuple(shape), lambda *_: index, pipeline_mode=pl.Buffered(1))


def _rms(x):
    return x * lax.rsqrt(jnp.mean(x * x, axis=-1, keepdims=True) + EPS)


def _gelu(x):
    return 0.5 * x * (1.0 + jnp.tanh(0.7978845608028654 * (x + 0.044715 * (x * x * x))))


def _sigmoid(x):
    return 1.0 / (1.0 + jnp.exp(-x))


def _in_proj_kernel(x_ref, g_ref, w_ref, qkv_ref, rest_ref):
    h = (_rms(x_ref[...]) * g_ref[...]).astype(BF16)
    step = 512
    for c in range(IN_WIDTH // step):
        p = jnp.dot(h, w_ref[:, c * step:(c + 1) * step], preferred_element_type=F32)
        if c == 0:
            qkv_ref[:, 0:step] = (p * (SB_HEAD_DIM ** -0.5)).astype(BF16)
        elif c < QKV_WIDTH // step:
            qkv_ref[:, c * step:(c + 1) * step] = p.astype(BF16)
        else:
            rest_ref[:, c * step - QKV_WIDTH:(c + 1) * step - QKV_WIDTH] = p


def _in_proj(l, x2, g, w_bf16):
    t = x2.shape[0]
    return pl.pallas_call(
        _in_proj_kernel,
        grid=(t // ROW_TILE,),
        in_specs=[
            pl.BlockSpec((ROW_TILE, D_MODEL), lambda i: (i, 0)),
            _layer_spec(l, (1, D_MODEL)),
            _layer_spec(l, (D_MODEL, IN_WIDTH)),
        ],
        out_specs=[
            pl.BlockSpec((ROW_TILE, QKV_WIDTH), lambda i: (i, 0)),
            pl.BlockSpec((ROW_TILE, REST_WIDTH), lambda i: (i, 0)),
        ],
        out_shape=[
            jax.ShapeDtypeStruct((t, QKV_WIDTH), BF16),
            jax.ShapeDtypeStruct((t, REST_WIDTH), F32),
        ],
        compiler_params=_params("parallel"),
        name="in_proj",
    )(x2, g, w_bf16)


def _sb_attn_kernel(q_ref, k_ref, v_ref, o_ref):
    i = pl.program_id(1)
    blk = SB_BLOCK
    pairs = SB_WIDTH // LANES
    row = lax.broadcasted_iota(jnp.int32, (blk, 2 * blk), 0)
    col = lax.broadcasted_iota(jnp.int32, (blk, 2 * blk), 1)
    causal = (col % blk) < row
    trow = lax.broadcasted_iota(jnp.int32, (2 * blk, 2 * blk), 0)
    tcol = lax.broadcasted_iota(jnp.int32, (2 * blk, 2 * blk), 1)
    suffix_neg_ones = -jnp.logical_and(trow // blk == tcol // blk, trow >= tcol).astype(BF16)
    head0 = lax.broadcasted_iota(jnp.int32, (blk, LANES), 1) < SB_HEAD_DIM

    def split_heads(t):
        zero = jnp.zeros_like(t)
        return jnp.concatenate([jnp.where(head0, t, zero), jnp.where(head0, zero, t)], axis=0)

    def key_rows(j):
        return pl.ds(pl.multiple_of(j * blk, blk), blk)

    def scores(p, j, allowed):
        cols = slice(p * LANES, (p + 1) * LANES)
        kk = split_heads(k_ref[key_rows(j), cols])
        s = lax.dot_general(q_ref[:, cols], kk, (((1,), (1,)), ((), ())), preferred_element_type=F32)
        if allowed is not None:
            s = jnp.where(allowed, s, SB_MASKED_SCORE)
        return s

    def suffix_sum(s):
        neg_ls = jnp.maximum(s, 0.0) + jnp.log(1.0 + jnp.exp2(jnp.abs(s) * (-LOG2_E)))
        return jnp.dot(neg_ls.astype(BF16), suffix_neg_ones, preferred_element_type=F32)

    def weighted_values(p, j, s, incl, carry):
        cols = slice(p * LANES, (p + 1) * LANES)
        vv = split_heads(v_ref[key_rows(j), cols])
        c0, c1 = carry
        carry_b = jnp.concatenate([jnp.broadcast_to(c0, (blk, blk)), jnp.broadcast_to(c1, (blk, blk))], axis=1)
        w = jnp.exp2((s + incl + carry_b) * LOG2_E)
        return jnp.dot(w.astype(BF16), vv, preferred_element_type=F32)

    def key_blocks(blocks, carries, accs):
        tiles = [(j, p, allowed) for j, allowed in blocks for p in range(pairs)]
        s_all = [scores(p, j, allowed) for j, p, allowed in tiles]
        incl_all = [suffix_sum(s) for s in s_all]
        carries, accs = list(carries), list(accs)
        for (j, p, _), s, incl in zip(tiles, s_all, incl_all):
            accs[p] = accs[p] + weighted_values(p, j, s, incl, carries[p])
            carries[p] = (carries[p][0] + incl[:, 0:1], carries[p][1] + incl[:, blk:blk + 1])
        return tuple(carries), tuple(accs)

    zero_col = jnp.zeros((blk, 1), F32)
    carries = tuple((zero_col, zero_col) for _ in range(pairs))
    accs = tuple(jnp.zeros((blk, LANES), F32) for _ in range(pairs))
    static_blocks = [(i, causal)] + [(jnp.maximum(i - d, 0), i - d >= 0) for d in range(1, SB_STATIC_BLOCKS)]
    carries, accs = key_blocks(static_blocks, carries, accs)

    def cond(state):
        j, carries, _ = state
        live = functools.reduce(jnp.maximum, [c for pair in carries for c in pair])
        return jnp.logical_and(j >= 0, jnp.max(live) > SB_DEAD_LOG)

    def body(state):
        j, carries, accs = state
        carries, accs = key_blocks([(j, None)], carries, accs)
        return j - 1, carries, accs

    _, _, accs = lax.while_loop(cond, body, (i - SB_STATIC_BLOCKS, carries, accs))
    for p in range(pairs):
        o_ref[:, p * LANES:(p + 1) * LANES] = accs[p]


def _sb_attention(qkv3):
    b, s, _ = qkv3.shape
    return pl.pallas_call(
        _sb_attn_kernel,
        grid=(b, s // SB_BLOCK),
        in_specs=[
            pl.BlockSpec((None, SB_BLOCK, SB_WIDTH), lambda bi, i: (bi, i, 0)),
            pl.BlockSpec((None, s, SB_WIDTH), lambda bi, i: (bi, 0, 1)),
            pl.BlockSpec((None, s, SB_WIDTH), lambda bi, i: (bi, 0, 2)),
        ],
        out_specs=pl.BlockSpec((None, SB_BLOCK, SB_WIDTH), lambda bi, i: (bi, i, 0)),
        out_shape=jax.ShapeDtypeStruct((b, s, SB_WIDTH), F32),
        compiler_params=_params("parallel", "parallel"),
        name="sb_attn",
    )(qkv3, qkv3, qkv3)


def _mix_kernel(rest_ref, halo_ref, cw_ref, cb_ref, wa_ref, ba_ref, wx_ref, bx_ref, lam_ref,
                sgn_ref, sgw_ref, sgb_ref, b_ref, c_ref, xpad_ref, h_ref):
    i = pl.program_id(1)
    tm = MIX_TILE
    w = LRU_WIDTH

    @pl.when(i == 0)
    def _():
        h_ref[...] = jnp.zeros_like(h_ref)

    xr = rest_ref[:, 0:w]
    xpad_ref[0:SUBLANES, :] = jnp.where(i == 0, 0.0, halo_ref[...])
    xpad_ref[SUBLANES:SUBLANES + tm, :] = xr
    xc = cb_ref[...] + cw_ref[LRU_CONV - 1:LRU_CONV, :] * xr
    for k in range(LRU_CONV - 1):
        off = SUBLANES - (LRU_CONV - 1) + k
        xc = xc + cw_ref[k:k + 1, :] * xpad_ref[off:off + tm, :]

    xcb = xc.astype(BF16)
    r = _sigmoid(jnp.dot(xcb, wa_ref[...], preferred_element_type=F32) + ba_ref[...])
    gate_i = _sigmoid(jnp.dot(xcb, wx_ref[...], preferred_element_type=F32) + bx_ref[...])
    lam = lam_ref[...]
    softplus_neg_lam = jnp.maximum(-lam, 0.0) + jnp.log1p(jnp.exp(-jnp.abs(lam)))
    log_a = (-LRU_C) * r * softplus_neg_lam
    a = jnp.exp(log_a)
    u = jnp.sqrt(-jnp.tanh(log_a) * (a * a + 1.0)) * (gate_i * xc)

    row = lax.broadcasted_iota(jnp.int32, (tm, w), 0)
    d = 1
    while d < tm:
        if d < SUBLANES:
            valid = row >= d
            a_sh = jnp.where(valid, pltpu.roll(a, d, 0), 1.0)
            u_sh = jnp.where(valid, pltpu.roll(u, d, 0), 0.0)
        else:
            a_sh = jnp.concatenate([jnp.ones((d, w), F32), a[:tm - d]], axis=0)
            u_sh = jnp.concatenate([jnp.zeros((d, w), F32), u[:tm - d]], axis=0)
        u = a * u_sh + u
        a = a * a_sh
        d *= 2
    hcur = u + a * h_ref[0:1, :]
    h_ref[...] = jnp.broadcast_to(hcur[tm - 1:tm, :], h_ref.shape)
    b_ref[...] = hcur * _gelu(rest_ref[:, w:2 * w])

    su = _gelu(rest_ref[:, 2 * w:3 * w])
    sv = _gelu(rest_ref[:, 3 * w:4 * w])
    vn = (_rms(sv) * sgn_ref[...]).astype(BF16)
    pi = lax.broadcasted_iota(jnp.int32, (SG_CHUNK, SG_CHUNK), 0)
    pj = lax.broadcasted_iota(jnp.int32, (SG_CHUNK, SG_CHUNK), 1)
    chunk_causal = (pj // CHUNK) <= (pi // CHUNK)
    lane = lax.broadcasted_iota(jnp.int32, (SG_CHUNK, SG_WIDTH), 1)
    wm = [jnp.where(chunk_causal, sgw_ref[g], 0.0).astype(BF16) for g in range(SG_GROUPS)]
    for n in range(tm // SG_CHUNK):
        vchunk = vn[n * SG_CHUNK:(n + 1) * SG_CHUNK, :]
        mixed = sgb_ref[...]
        for g in range(SG_GROUPS):
            mg = jnp.dot(wm[g], vchunk, preferred_element_type=F32)
            in_group = (lane // SG_GROUP_DIM) == g
            mixed = mixed + jnp.where(in_group, mg, 0.0)
        c_ref[n * SG_CHUNK:(n + 1) * SG_CHUNK, :] = su[n * SG_CHUNK:(n + 1) * SG_CHUNK, :] * mixed


def _mix_branches(l, rest3, cw, cb, wa_bd, ba, wx_bd, bx, lam, sgn, sgw, sgb_full):
    b, s, _ = rest3.shape
    tm = MIX_TILE
    w = LRU_WIDTH
    halo_blocks = tm // SUBLANES
    return pl.pallas_call(
        _mix_kernel,
        grid=(b, s // tm),
        in_specs=[
            pl.BlockSpec((None, tm, REST_WIDTH), lambda bi, i: (bi, i, 0)),
            pl.BlockSpec((None, SUBLANES, w), lambda bi, i: (bi, jnp.maximum(i * halo_blocks - 1, 0), 0)),
            _layer_spec(l, (LRU_CONV, w)),
            _layer_spec(l, (1, w)),
            _layer_spec(l, (w, w)),
            _layer_spec(l, (1, w)),
            _layer_spec(l, (w, w)),
            _layer_spec(l, (1, w)),
            _layer_spec(l, (1, w)),
            _layer_spec(l, (1, SG_WIDTH)),
            _layer_spec(l, (SG_GROUPS, SG_CHUNK, SG_CHUNK)),
            _layer_spec(l, (SG_CHUNK, SG_WIDTH)),
        ],
        out_specs=[
            pl.BlockSpec((None, tm, w), lambda bi, i: (bi, i, 0)),
            pl.BlockSpec((None, tm, SG_WIDTH), lambda bi, i: (bi, i, 0)),
        ],
        out_shape=[
            jax.ShapeDtypeStruct((b, s, w), F32),
            jax.ShapeDtypeStruct((b, s, SG_WIDTH), F32),
        ],
        scratch_shapes=[
            pltpu.VMEM((tm + SUBLANES, w), F32),
            pltpu.VMEM((SUBLANES, w), F32),
        ],
        compiler_params=_params("parallel", "arbitrary"),
        name="mix_branches",
    )(rest3, rest3, cw, cb, wa_bd, ba, wx_bd, bx, lam, sgn, sgw, sgb_full)


def _mem_kv_kernel(mem_ref, g_ref, w_ref, kv_ref):
    mn = (_rms(mem_ref[...]) * g_ref[...]).astype(BF16)
    kv_ref[...] = jnp.dot(mn, w_ref[...], preferred_element_type=F32).astype(BF16)


def _mem_kv(l, mem, g, w_bf16):
    b, m, _ = mem.shape
    return pl.pallas_call(
        _mem_kv_kernel,
        grid=(b,),
        in_specs=[
            pl.BlockSpec((None, m, D_MODEL), lambda bi: (bi, 0, 0)),
            _layer_spec(l, (1, D_MODEL)),
            _layer_spec(l, (D_MODEL, 2 * D_MODEL)),
        ],
        out_specs=pl.BlockSpec((None, m, 2 * D_MODEL), lambda bi: (bi, 0, 0)),
        out_shape=jax.ShapeDtypeStruct((b, m, 2 * D_MODEL), BF16),
        compiler_params=_params("parallel"),
        name="mem_kv",
    )(mem, g, w_bf16)


def _mix_out_xattn_kernel(x_ref, a_ref, b_ref, c_ref, mn_ref, wout_ref, gmix_ref, gpre_ref, wq_ref,
                          kv_ref, wo_ref, gpost_ref, o_ref):
    groups = [slice(g * ROW_TILE, (g + 1) * ROW_TILE) for g in range(XA_TILE // ROW_TILE)]
    heads = [slice(hd * XA_HEAD_DIM, (hd + 1) * XA_HEAD_DIM) for hd in range(XA_HEADS)]
    nt = (((1,), (1,)), ((), ()))

    mixed = [(jnp.concatenate([_rms(a_ref[r, :]), _rms(b_ref[r, :]), _rms(c_ref[r, :])], axis=-1)
              * mn_ref[...]).astype(BF16) for r in groups]
    y = [jnp.dot(m, wout_ref[...], preferred_element_type=F32) for m in mixed]
    x1 = [x_ref[r, :] + _rms(yg) * gmix_ref[...] for r, yg in zip(groups, y)]
    h = [(_rms(xg) * gpre_ref[...]).astype(BF16) for xg in x1]
    q = [(jnp.dot(hg, wq_ref[...], preferred_element_type=F32) * (XA_HEAD_DIM ** -0.5)).astype(BF16)
         for hg in h]
    s = [[lax.dot_general(qg[:, hs], kv_ref[:, hs], nt, preferred_element_type=F32) for hs in heads]
         for qg in q]
    e = [[jnp.exp(sh - jnp.max(sh, axis=-1, keepdims=True)) for sh in sg] for sg in s]
    p = [[(eh / jnp.sum(eh, axis=-1, keepdims=True)).astype(BF16) for eh in eg] for eg in e]
    o = [jnp.concatenate(
        [jnp.dot(ph, kv_ref[:, D_MODEL + hs.start:D_MODEL + hs.stop], preferred_element_type=F32).astype(BF16)
         for ph, hs in zip(pg, heads)], axis=-1) for pg in p]
    y2 = [jnp.dot(og, wo_ref[...], preferred_element_type=F32) for og in o]
    for r, xg, yg in zip(groups, x1, y2):
        o_ref[r, :] = xg + _rms(yg) * gpost_ref[...]


def _mix_out_xattn(l, x3, a3, b3, c3, mn, wout_bf16, gmix, gpre, wq_bf16, kv, wo_bf16, gpost):
    b, s, _ = x3.shape
    m = kv.shape[1]
    tm = XA_TILE
    rows = lambda width: pl.BlockSpec((None, tm, width), lambda bi, i: (bi, i, 0))
    return pl.pallas_call(
        _mix_out_xattn_kernel,
        grid=(b, s // tm),
        in_specs=[
            rows(D_MODEL), rows(SB_WIDTH), rows(LRU_WIDTH), rows(SG_WIDTH),
            _layer_spec(l, (1, D_MODEL)),
            _layer_spec(l, (D_MODEL, D_MODEL)),
            _layer_spec(l, (1, D_MODEL)),
            _layer_spec(l, (1, D_MODEL)),
            _layer_spec(l, (D_MODEL, D_MODEL)),
            pl.BlockSpec((None, m, 2 * D_MODEL), lambda bi, i: (bi, 0, 0)),
            _layer_spec(l, (D_MODEL, D_MODEL)),
            _layer_spec(l, (1, D_MODEL)),
        ],
        out_specs=rows(D_MODEL),
        out_shape=jax.ShapeDtypeStruct((b, s, D_MODEL), F32),
        compiler_params=_params("parallel", "parallel"),
        name="mix_out_xattn",
    )(x3, a3, b3, c3, mn, wout_bf16, gmix, gpre, wq_bf16, kv, wo_bf16, gpost)


def _ffn_kernel(x_ref, halo_ref, gpre_ref, wup_ref, cw_ref, cb_ref, wdown_ref, gpost_ref, o_ref,
                z_ref, act_ref):
    i = pl.program_id(1)
    tm = ROW_TILE
    x = x_ref[...]
    xh = jnp.concatenate([halo_ref[...], x], axis=0)
    h = (_rms(xh) * gpre_ref[...]).astype(BF16)

    def up_proj(col):
        return jnp.dot(h, wup_ref[:, col:col + FF_CHUNK], preferred_element_type=F32)

    def conv(z, col, slot):
        halves = []
        for hf in range(FF_CHUNK // LANES):
            lanes = slice(hf * LANES, (hf + 1) * LANES)
            wcol = slice(col + hf * LANES, col + (hf + 1) * LANES)
            z_ref[slot, hf, 0:SUBLANES, :] = jnp.where(i == 0, 0.0, z[0:SUBLANES, lanes])
            z_ref[slot, hf, SUBLANES:, :] = z[SUBLANES:, lanes]
            out = cb_ref[:, wcol] + cw_ref[FFN_CONV - 1:FFN_CONV, wcol] * z[SUBLANES:, lanes]
            for k in range(FFN_CONV - 1):
                off = SUBLANES - (FFN_CONV - 1) + k
                out = out + cw_ref[k:k + 1, wcol] * z_ref[slot, hf, off:off + tm, :]
            halves.append(out)
        return jnp.concatenate(halves, axis=-1)

    for c in range(D_FF // FF_CHUNK):
        gate = conv(up_proj(c * FF_CHUNK), c * FF_CHUNK, (2 * c) % FFN_Z_SLOTS)
        up = conv(up_proj(D_FF + c * FF_CHUNK), D_FF + c * FF_CHUNK, (2 * c + 1) % FFN_Z_SLOTS)
        act_ref[:, c * FF_CHUNK:(c + 1) * FF_CHUNK] = (_gelu(gate) * up).astype(BF16)
    y = jnp.dot(act_ref[...], wdown_ref[...], preferred_element_type=F32)
    o_ref[...] = x + _rms(y) * gpost_ref[...]


def _ffn(l, x3, gpre, wup_bf16, cw, cb, wdown_bf16, gpost):
    b, s, _ = x3.shape
    tm = ROW_TILE
    halo_blocks = tm // SUBLANES
    return pl.pallas_call(
        _ffn_kernel,
        grid=(b, s // tm),
        in_specs=[
            pl.BlockSpec((None, tm, D_MODEL), lambda bi, i: (bi, i, 0)),
            pl.BlockSpec((None, SUBLANES, D_MODEL),
                         lambda bi, i: (bi, jnp.maximum(i * halo_blocks - 1, 0), 0)),
            _layer_spec(l, (1, D_MODEL)),
            _layer_spec(l, (D_MODEL, 2 * D_FF)),
            _layer_spec(l, (FFN_CONV, 2 * D_FF)),
            _layer_spec(l, (1, 2 * D_FF)),
            _layer_spec(l, (D_FF, D_MODEL)),
            _layer_spec(l, (1, D_MODEL)),
        ],
        out_specs=pl.BlockSpec((None, tm, D_MODEL), lambda bi, i: (bi, i, 0)),
        out_shape=jax.ShapeDtypeStruct((b, s, D_MODEL), F32),
        scratch_shapes=[pltpu.VMEM((FFN_Z_SLOTS, FF_CHUNK // LANES, tm + SUBLANES, LANES), F32),
                        pltpu.VMEM((tm, D_FF), BF16)],
        compiler_params=_params("parallel", "parallel"),
        name="ffn",
    )(x3, x3, gpre, wup_bf16, cw, cb, wdown_bf16, gpost)


def _block_diag(w):
    layers, n, c, d = w.shape
    eye = jnp.eye(n, dtype=w.dtype)
    return (eye[None, :, None, :, None] * w[:, :, :, None, :]).reshape(layers, n * c, n * d)


def kernel(x, mem, norm_mix_pre, w_in, lru_conv_w, lru_conv_b, lru_w_a, lru_b_a, lru_w_x, lru_b_x, lru_lambda, sg_norm, sg_w, sg_b, mix_norm, w_out, norm_mix_post, norm_xa_pre, norm_mem, xa_w_q, xa_w_kv, xa_w_o, norm_xa_post, norm_ffn_pre, ffn_w_up, ffn_conv_w, ffn_conv_b, ffn_w_down, norm_ffn_post):
    b, s, d = x.shape
    depth = w_in.shape[0]
    t = b * s
    rows = lambda v: v.reshape(depth, 1, -1)
    w_in_b, w_out_b = w_in.astype(BF16), w_out.astype(BF16)
    wq_b, wkv_b, wo_b = xa_w_q.astype(BF16), xa_w_kv.astype(BF16), xa_w_o.astype(BF16)
    wup_b, wdown_b = ffn_w_up.astype(BF16), ffn_w_down.astype(BF16)
    wa_bd, wx_bd = _block_diag(lru_w_a).astype(BF16), _block_diag(lru_w_x).astype(BF16)
    sgb_full = jnp.repeat(jnp.swapaxes(sg_b, 1, 2), SG_GROUP_DIM, axis=2)

    for l in range(depth):
        qkv, rest = _in_proj(l, x.reshape(t, d), rows(norm_mix_pre), w_in_b)
        a_out = _sb_attention(qkv.reshape(b, s, QKV_WIDTH))
        b_out, c_out = _mix_branches(
            l, rest.reshape(b, s, REST_WIDTH), lru_conv_w, rows(lru_conv_b), wa_bd, rows(lru_b_a),
            wx_bd, rows(lru_b_x), rows(lru_lambda), rows(sg_norm), sg_w, sgb_full)
        kv = _mem_kv(l, mem, rows(norm_mem), wkv_b)
        x3 = _mix_out_xattn(l, x, a_out, b_out, c_out, rows(mix_norm), w_out_b, rows(norm_mix_post),
                            rows(norm_xa_pre), wq_b, kv, wo_b, rows(norm_xa_post))
        x = _ffn(l, x3, rows(norm_ffn_pre), wup_b, ffn_conv_w, rows(ffn_conv_b), wdown_b,
                 rows(norm_ffn_post))
    return x
```

```python
import functools

import jax
import jax.numpy as jnp
from jax import lax
from jax.experimental import pallas as pl
from jax.experimental.pallas import tpu as pltpu

F32 = jnp.float32
BF16 = jnp.bfloat16

D_MODEL = 1024
CHUNK = 64
EPS = 1e-6
SB_HEAD_DIM = 64
SB_WIDTH = 512
LRU_WIDTH = 256
LRU_BLOCKS = 4
LRU_CONV = 4
LRU_C = 8.0
SG_WIDTH = 256
SG_GROUPS = 4
SG_GROUP_DIM = 64
SG_CHUNK = 128
IN_WIDTH = 2560
QKV_WIDTH = 3 * SB_WIDTH
XA_HEADS = 4
XA_HEAD_DIM = 256
D_FF = 2816
FFN_CONV = 3

SUBLANES = 8
LANES = 128
VMEM_LIMIT_BYTES = 56 * 1024 * 1024

ROW_TILE = 512
XA_TILE = 1024
SB_BLOCK = 128
FF_CHUNK = 256
FFN_Z_SLOTS = 4
SB_DEAD_LOG = -110.0
LOG2_E = 1.4426950408889634
SB_MASKED_SCORE = -1e30
SB_STATIC_BLOCKS = 3


def _params(*semantics):
    return pltpu.CompilerParams(dimension_semantics=semantics, vmem_limit_bytes=VMEM_LIMIT_BYTES)


def _layer_spec(l, shape):
    index = (l,) + (0,) * len(shape)
    return pl.BlockSpec((None,) + tuple(shape), lambda *_: index, pipeline_mode=pl.Buffered(1))


def _rms(x):
    return x * lax.rsqrt(jnp.mean(x * x, axis=-1, keepdims=True) + EPS)


def _gelu(x):
    return 0.5 * x * (1.0 + jnp.tanh(0.7978845608028654 * (x + 0.044715 * (x * x * x))))


def _sigmoid(x):
    return 1.0 / (1.0 + jnp.exp(-x))


def _in_mix_kernel(x_ref, g_ref, w_ref, cw_ref, cb_ref, wa_ref, ba_ref, wx_ref, bx_ref, lam_ref,
                   sgn_ref, sgw_ref, sgb_ref, qkv_ref, b_ref, c_ref, xpad_ref, h_ref):
    i = pl.program_id(1)
    tm = ROW_TILE
    w = LRU_WIDTH

    @pl.when(i == 0)
    def _():
        h_ref[...] = jnp.zeros_like(h_ref)
        xpad_ref[0:SUBLANES, :] = jnp.zeros((SUBLANES, w), F32)

    h = (_rms(x_ref[...]) * g_ref[...]).astype(BF16)

    def proj(col, width):
        return jnp.dot(h, w_ref[:, col:col + width], preferred_element_type=F32)

    lru_in = proj(QKV_WIDTH, 2 * w)
    sg_in = proj(QKV_WIDTH + 2 * w, 2 * SG_WIDTH)

    xr = lru_in[:, 0:w]
    xpad_ref[SUBLANES:SUBLANES + tm, :] = xr
    xc = cb_ref[...] + cw_ref[LRU_CONV - 1:LRU_CONV, :] * xr
    for k in range(LRU_CONV - 1):
        off = SUBLANES - (LRU_CONV - 1) + k
        xc = xc + cw_ref[k:k + 1, :] * xpad_ref[off:off + tm, :]
    xpad_ref[0:SUBLANES, :] = xr[tm - SUBLANES:tm, :]

    qkv_ref[:, 0:SB_WIDTH] = (proj(0, SB_WIDTH) * (SB_HEAD_DIM ** -0.5)).astype(BF16)

    xcb = xc.astype(BF16)
    r = _sigmoid(jnp.dot(xcb, wa_ref[...], preferred_element_type=F32) + ba_ref[...])
    gate_i = _sigmoid(jnp.dot(xcb, wx_ref[...], preferred_element_type=F32) + bx_ref[...])

    qkv_ref[:, SB_WIDTH:2 * SB_WIDTH] = proj(SB_WIDTH, SB_WIDTH).astype(BF16)
    qkv_ref[:, 2 * SB_WIDTH:3 * SB_WIDTH] = proj(2 * SB_WIDTH, SB_WIDTH).astype(BF16)

    lam = lam_ref[...]
    softplus_neg_lam = jnp.maximum(-lam, 0.0) + jnp.log1p(jnp.exp(-jnp.abs(lam)))
    log_a = (-LRU_C) * r * softplus_neg_lam
    a = jnp.exp(log_a)
    u = jnp.sqrt(-jnp.tanh(log_a) * (a * a + 1.0)) * (gate_i * xc)

    y_gate = _gelu(lru_in[:, w:2 * w])
    slab_row = lax.broadcasted_iota(jnp.int32, (SUBLANES, w), 0)
    h_prev = h_ref[0:1, :]
    for k in range(tm // SUBLANES):
        rows8 = slice(k * SUBLANES, (k + 1) * SUBLANES)
        a8, u8 = a[rows8], u[rows8]
        d = 1
        while d < SUBLANES:
            valid = slab_row >= d
            u8 = a8 * jnp.where(valid, pltpu.roll(u8, d, 0), 0.0) + u8
            a8 = a8 * jnp.where(valid, pltpu.roll(a8, d, 0), 1.0)
            d *= 2
        h8 = u8 + a8 * h_prev
        b_ref[rows8, :] = h8 * y_gate[rows8]
        h_prev = h8[SUBLANES - 1:SUBLANES, :]
    h_ref[...] = jnp.broadcast_to(h_prev, h_ref.shape)

    su = _gelu(sg_in[:, 0:SG_WIDTH])
    sv = _gelu(sg_in[:, SG_WIDTH:2 * SG_WIDTH])
    vn = (_rms(sv) * sgn_ref[...]).astype(BF16)
    pi = lax.broadcasted_iota(jnp.int32, (SG_CHUNK, SG_CHUNK), 0)
    pj = lax.broadcasted_iota(jnp.int32, (SG_CHUNK, SG_CHUNK), 1)
    chunk_causal = (pj // CHUNK) <= (pi // CHUNK)
    lane = lax.broadcasted_iota(jnp.int32, (SG_CHUNK, SG_WIDTH), 1)
    wm = jnp.concatenate([jnp.where(chunk_causal, sgw_ref[g], 0.0).astype(BF16) for g in range(SG_GROUPS)],
                         axis=0)
    for n in range(tm // SG_CHUNK):
        chunk = slice(n * SG_CHUNK, (n + 1) * SG_CHUNK)
        mg = jnp.dot(wm, vn[chunk, :], preferred_element_type=F32)
        mixed = mg[0:SG_CHUNK]
        for g in range(1, SG_GROUPS):
            mixed = jnp.where(lane >= g * SG_GROUP_DIM, mg[g * SG_CHUNK:(g + 1) * SG_CHUNK], mixed)
        c_ref[chunk, :] = su[chunk, :] * (mixed + sgb_ref[...])


def _in_mix(l, x3, g, w_bf16, cw, cb, wa_bd, ba, wx_bd, bx, lam, sgn, sgw, sgb_full):
    b, s, _ = x3.shape
    tm = ROW_TILE
    w = LRU_WIDTH
    rows = lambda width: pl.BlockSpec((None, tm, width), lambda bi, i: (bi, i, 0))
    return pl.pallas_call(
        _in_mix_kernel,
        grid=(b, s // tm),
        in_specs=[
            rows(D_MODEL),
            _layer_spec(l, (1, D_MODEL)),
            _layer_spec(l, (D_MODEL, IN_WIDTH)),
            _layer_spec(l, (LRU_CONV, w)),
            _layer_spec(l, (1, w)),
            _layer_spec(l, (w, w)),
            _layer_spec(l, (1, w)),
            _layer_spec(l, (w, w)),
            _layer_spec(l, (1, w)),
            _layer_spec(l, (1, w)),
            _layer_spec(l, (1, SG_WIDTH)),
            _layer_spec(l, (SG_GROUPS, SG_CHUNK, SG_CHUNK)),
            _layer_spec(l, (SG_CHUNK, SG_WIDTH)),
        ],
        out_specs=[rows(QKV_WIDTH), rows(w), rows(SG_WIDTH)],
        out_shape=[
            jax.ShapeDtypeStruct((b, s, QKV_WIDTH), BF16),
            jax.ShapeDtypeStruct((b, s, w), F32),
            jax.ShapeDtypeStruct((b, s, SG_WIDTH), F32),
        ],
        scratch_shapes=[
            pltpu.VMEM((tm + SUBLANES, w), F32),
            pltpu.VMEM((SUBLANES, w), F32),
        ],
        compiler_params=_params("parallel", "arbitrary"),
        name="in_mix",
    )(x3, g, w_bf16, cw, cb, wa_bd, ba, wx_bd, bx, lam, sgn, sgw, sgb_full)


def _sb_attn_kernel(q_ref, k_ref, v_ref, o_ref):
    i = pl.program_id(1)
    blk = SB_BLOCK
    pairs = SB_WIDTH // LANES
    row = lax.broadcasted_iota(jnp.int32, (blk, 2 * blk), 0)
    col = lax.broadcasted_iota(jnp.int32, (blk, 2 * blk), 1)
    causal = (col % blk) < row
    trow = lax.broadcasted_iota(jnp.int32, (2 * blk, 2 * blk), 0)
    tcol = lax.broadcasted_iota(jnp.int32, (2 * blk, 2 * blk), 1)
    suffix_neg_ones = -jnp.logical_and(trow // blk == tcol // blk, trow >= tcol).astype(BF16)
    head0 = lax.broadcasted_iota(jnp.int32, (blk, LANES), 1) < SB_HEAD_DIM

    def split_heads(t):
        zero = jnp.zeros_like(t)
        return jnp.concatenate([jnp.where(head0, t, zero), jnp.where(head0, zero, t)], axis=0)

    def key_rows(j):
        return pl.ds(pl.multiple_of(j * blk, blk), blk)

    def scores(p, j, allowed):
        cols = slice(p * LANES, (p + 1) * LANES)
        kk = split_heads(k_ref[key_rows(j), cols])
        s = lax.dot_general(q_ref[:, cols], kk, (((1,), (1,)), ((), ())), preferred_element_type=F32)
        if allowed is not None:
            s = jnp.where(allowed, s, SB_MASKED_SCORE)
        return s

    def suffix_sum(s):
        neg_ls = jnp.maximum(s, 0.0) + jnp.log(1.0 + jnp.exp2(jnp.abs(s) * (-LOG2_E)))
        return jnp.dot(neg_ls.astype(BF16), suffix_neg_ones, preferred_element_type=F32)

    def weighted_values(p, j, s, incl, carry):
        cols = slice(p * LANES, (p + 1) * LANES)
        vv = split_heads(v_ref[key_rows(j), cols])
        c0, c1 = carry
        carry_b = jnp.concatenate([jnp.broadcast_to(c0, (blk, blk)), jnp.broadcast_to(c1, (blk, blk))], axis=1)
        w = jnp.exp2((s + incl + carry_b) * LOG2_E)
        return jnp.dot(w.astype(BF16), vv, preferred_element_type=F32)

    def key_blocks(blocks, carries, accs):
        tiles = [(j, p, allowed) for j, allowed in blocks for p in range(pairs)]
        s_all = [scores(p, j, allowed) for j, p, allowed in tiles]
        incl_all = [suffix_sum(s) for s in s_all]
        carries, accs = list(carries), list(accs)
        for (j, p, _), s, incl in zip(tiles, s_all, incl_all):
            accs[p] = accs[p] + weighted_values(p, j, s, incl, carries[p])
            carries[p] = (carries[p][0] + incl[:, 0:1], carries[p][1] + incl[:, blk:blk + 1])
        return tuple(carries), tuple(accs)

    zero_col = jnp.zeros((blk, 1), F32)
    carries = tuple((zero_col, zero_col) for _ in range(pairs))
    accs = tuple(jnp.zeros((blk, LANES), F32) for _ in range(pairs))
    static_blocks = [(i, causal)] + [(jnp.maximum(i - d, 0), i - d >= 0) for d in range(1, SB_STATIC_BLOCKS)]
    carries, accs = key_blocks(static_blocks, carries, accs)

    def cond(state):
        j, carries, _ = state
        live = functools.reduce(jnp.maximum, [c for pair in carries for c in pair])
        return jnp.logical_and(j >= 0, jnp.max(live) > SB_DEAD_LOG)

    def body(state):
        j, carries, accs = state
        carries, accs = key_blocks([(j, None)], carries, accs)
        return j - 1, carries, accs

    _, _, accs = lax.while_loop(cond, body, (i - SB_STATIC_BLOCKS, carries, accs))
    for p in range(pairs):
        o_ref[:, p * LANES:(p + 1) * LANES] = accs[p]


def _sb_attention(qkv3):
    b, s, _ = qkv3.shape
    return pl.pallas_call(
        _sb_attn_kernel,
        grid=(b, s // SB_BLOCK),
        in_specs=[
            pl.BlockSpec((None, SB_BLOCK, SB_WIDTH), lambda bi, i: (bi, i, 0)),
            pl.BlockSpec((None, s, SB_WIDTH), lambda bi, i: (bi, 0, 1)),
            pl.BlockSpec((None, s, SB_WIDTH), lambda bi, i: (bi, 0, 2)),
        ],
        out_specs=pl.BlockSpec((None, SB_BLOCK, SB_WIDTH), lambda bi, i: (bi, i, 0)),
        out_shape=jax.ShapeDtypeStruct((b, s, SB_WIDTH), F32),
        compiler_params=_params("parallel", "parallel"),
        name="sb_attn",
    )(qkv3, qkv3, qkv3)


def _mem_kv_kernel(mem_ref, g_ref, w_ref, kv_ref):
    mn = (_rms(mem_ref[...]) * g_ref[...]).astype(BF16)
    kv_ref[...] = jnp.dot(mn, w_ref[...], preferred_element_type=F32).astype(BF16)


def _mem_kv(l, mem, g, w_bf16):
    b, m, _ = mem.shape
    return pl.pallas_call(
        _mem_kv_kernel,
        grid=(b,),
        in_specs=[
            pl.BlockSpec((None, m, D_MODEL), lambda bi: (bi, 0, 0)),
            _layer_spec(l, (1, D_MODEL)),
            _layer_spec(l, (D_MODEL, 2 * D_MODEL)),
        ],
        out_specs=pl.BlockSpec((None, m, 2 * D_MODEL), lambda bi: (bi, 0, 0)),
        out_shape=jax.ShapeDtypeStruct((b, m, 2 * D_MODEL), BF16),
        compiler_params=_params("parallel"),
        name="mem_kv",
    )(mem, g, w_bf16)


def _mix_out_xattn_kernel(x_ref, a_ref, b_ref, c_ref, mn_ref, wout_ref, gmix_ref, gpre_ref, wq_ref,
                          kv_ref, wo_ref, gpost_ref, o_ref):
    groups = [slice(g * ROW_TILE, (g + 1) * ROW_TILE) for g in range(XA_TILE // ROW_TILE)]
    heads = [slice(hd * XA_HEAD_DIM, (hd + 1) * XA_HEAD_DIM) for hd in range(XA_HEADS)]
    nt = (((1,), (1,)), ((), ()))

    mixed = [(jnp.concatenate([_rms(a_ref[r, :]), _rms(b_ref[r, :]), _rms(c_ref[r, :])], axis=-1)
              * mn_ref[...]).astype(BF16) for r in groups]
    y = [jnp.dot(m, wout_ref[...], preferred_element_type=F32) for m in mixed]
    x1 = [x_ref[r, :] + _rms(yg) * gmix_ref[...] for r, yg in zip(groups, y)]
    h = [(_rms(xg) * gpre_ref[...]).astype(BF16) for xg in x1]
    q = [(jnp.dot(hg, wq_ref[...], preferred_element_type=F32) * (XA_HEAD_DIM ** -0.5)).astype(BF16)
         for hg in h]
    s = [[lax.dot_general(qg[:, hs], kv_ref[:, hs], nt, preferred_element_type=F32) for hs in heads]
         for qg in q]
    e = [[jnp.exp(sh - jnp.max(sh, axis=-1, keepdims=True)) for sh in sg] for sg in s]
    p = [[(eh / jnp.sum(eh, axis=-1, keepdims=True)).astype(BF16) for eh in eg] for eg in e]
    o = [jnp.concatenate(
        [jnp.dot(ph, kv_ref[:, D_MODEL + hs.start:D_MODEL + hs.stop], preferred_element_type=F32).astype(BF16)
         for ph, hs in zip(pg, heads)], axis=-1) for pg in p]
    y2 = [jnp.dot(og, wo_ref[...], preferred_element_type=F32) for og in o]
    for r, xg, yg in zip(groups, x1, y2):
        o_ref[r, :] = xg + _rms(yg) * gpost_ref[...]


def _mix_out_xattn(l, x3, a3, b3, c3, mn, wout_bf16, gmix, gpre, wq_bf16, kv, wo_bf16, gpost):
    b, s, _ = x3.shape
    m = kv.shape[1]
    tm = XA_TILE
    rows = lambda width: pl.BlockSpec((None, tm, width), lambda bi, i: (bi, i, 0))
    return pl.pallas_call(
        _mix_out_xattn_kernel,
        grid=(b, s // tm),
        in_specs=[
            rows(D_MODEL), rows(SB_WIDTH), rows(LRU_WIDTH), rows(SG_WIDTH),
            _layer_spec(l, (1, D_MODEL)),
            _layer_spec(l, (D_MODEL, D_MODEL)),
            _layer_spec(l, (1, D_MODEL)),
            _layer_spec(l, (1, D_MODEL)),
            _layer_spec(l, (D_MODEL, D_MODEL)),
            pl.BlockSpec((None, m, 2 * D_MODEL), lambda bi, i: (bi, 0, 0)),
            _layer_spec(l, (D_MODEL, D_MODEL)),
            _layer_spec(l, (1, D_MODEL)),
        ],
        out_specs=rows(D_MODEL),
        out_shape=jax.ShapeDtypeStruct((b, s, D_MODEL), F32),
        compiler_params=_params("parallel", "parallel"),
        name="mix_out_xattn",
    )(x3, a3, b3, c3, mn, wout_bf16, gmix, gpre, wq_bf16, kv, wo_bf16, gpost)


def _ffn_kernel(x_ref, halo_ref, gpre_ref, wup_ref, cw_ref, cb_ref, wdown_ref, gpost_ref, o_ref,
                z_ref, act_ref):
    i = pl.program_id(1)
    tm = ROW_TILE
    x = x_ref[...]
    xh = jnp.concatenate([halo_ref[...], x], axis=0)
    h = (_rms(xh) * gpre_ref[...]).astype(BF16)

    def up_proj(col):
        return jnp.dot(h, wup_ref[:, col:col + FF_CHUNK], preferred_element_type=F32)

    def conv(z, col, slot):
        halves = []
        for hf in range(FF_CHUNK // LANES):
            lanes = slice(hf * LANES, (hf + 1) * LANES)
            wcol = slice(col + hf * LANES, col + (hf + 1) * LANES)
            z_ref[slot, hf, 0:SUBLANES, :] = jnp.where(i == 0, 0.0, z[0:SUBLANES, lanes])
            z_ref[slot, hf, SUBLANES:, :] = z[SUBLANES:, lanes]
            out = cb_ref[:, wcol] + cw_ref[FFN_CONV - 1:FFN_CONV, wcol] * z[SUBLANES:, lanes]
            for k in range(FFN_CONV - 1):
                off = SUBLANES - (FFN_CONV - 1) + k
                out = out + cw_ref[k:k + 1, wcol] * z_ref[slot, hf, off:off + tm, :]
            halves.append(out)
        return jnp.concatenate(halves, axis=-1)

    for c in range(D_FF // FF_CHUNK):
        gate = conv(up_proj(c * FF_CHUNK), c * FF_CHUNK, (2 * c) % FFN_Z_SLOTS)
        up = conv(up_proj(D_FF + c * FF_CHUNK), D_FF + c * FF_CHUNK, (2 * c + 1) % FFN_Z_SLOTS)
        act_ref[:, c * FF_CHUNK:(c + 1) * FF_CHUNK] = (_gelu(gate) * up).astype(BF16)
    y = jnp.dot(act_ref[...], wdown_ref[...], preferred_element_type=F32)
    o_ref[...] = x + _rms(y) * gpost_ref[...]


def _ffn(l, x3, gpre, wup_bf16, cw, cb, wdown_bf16, gpost):
    b, s, _ = x3.shape
    tm = ROW_TILE
    halo_blocks = tm // SUBLANES
    return pl.pallas_call(
        _ffn_kernel,
        grid=(b, s // tm),
        in_specs=[
            pl.BlockSpec((None, tm, D_MODEL), lambda bi, i: (bi, i, 0)),
            pl.BlockSpec((None, SUBLANES, D_MODEL),
                         lambda bi, i: (bi, jnp.maximum(i * halo_blocks - 1, 0), 0)),
            _layer_spec(l, (1, D_MODEL)),
            _layer_spec(l, (D_MODEL, 2 * D_FF)),
            _layer_spec(l, (FFN_CONV, 2 * D_FF)),
            _layer_spec(l, (1, 2 * D_FF)),
            _layer_spec(l, (D_FF, D_MODEL)),
            _layer_spec(l, (1, D_MODEL)),
        ],
        out_specs=pl.BlockSpec((None, tm, D_MODEL), lambda bi, i: (bi, i, 0)),
        out_shape=jax.ShapeDtypeStruct((b, s, D_MODEL), F32),
        scratch_shapes=[pltpu.VMEM((FFN_Z_SLOTS, FF_CHUNK // LANES, tm + SUBLANES, LANES), F32),
                        pltpu.VMEM((tm, D_FF), BF16)],
        compiler_params=_params("parallel", "parallel"),
        name="ffn",
    )(x3, x3, gpre, wup_bf16, cw, cb, wdown_bf16, gpost)


def _block_diag(w):
    layers, n, c, d = w.shape
    eye = jnp.eye(n, dtype=w.dtype)
    return (eye[None, :, None, :, None] * w[:, :, :, None, :]).reshape(layers, n * c, n * d)


def kernel(x, mem, norm_mix_pre, w_in, lru_conv_w, lru_conv_b, lru_w_a, lru_b_a, lru_w_x, lru_b_x, lru_lambda, sg_norm, sg_w, sg_b, mix_norm, w_out, norm_mix_post, norm_xa_pre, norm_mem, xa_w_q, xa_w_kv, xa_w_o, norm_xa_post, norm_ffn_pre, ffn_w_up, ffn_conv_w, ffn_conv_b, ffn_w_down, norm_ffn_post):
    depth = w_in.shape[0]
    rows = lambda v: v.reshape(depth, 1, -1)
    w_in_b, w_out_b = w_in.astype(BF16), w_out.astype(BF16)
    wq_b, wkv_b, wo_b = xa_w_q.astype(BF16), xa_w_kv.astype(BF16), xa_w_o.astype(BF16)
    wup_b, wdown_b = ffn_w_up.astype(BF16), ffn_w_down.astype(BF16)
    wa_bd, wx_bd = _block_diag(lru_w_a).astype(BF16), _block_diag(lru_w_x).astype(BF16)
    sgb_full = jnp.repeat(jnp.swapaxes(sg_b, 1, 2), SG_GROUP_DIM, axis=2)

    for l in range(depth):
        qkv, b_out, c_out = _in_mix(
            l, x, rows(norm_mix_pre), w_in_b, lru_conv_w, rows(lru_conv_b), wa_bd, rows(lru_b_a),
            wx_bd, rows(lru_b_x), rows(lru_lambda), rows(sg_norm), sg_w, sgb_full)
        a_out = _sb_attention(qkv)
        kv = _mem_kv(l, mem, rows(norm_mem), wkv_b)
        x3 = _mix_out_xattn(l, x, a_out, b_out, c_out, rows(mix_norm), w_out_b, rows(norm_mix_post),
                            rows(norm_xa_pre), wq_b, kv, wo_b, rows(norm_xa_post))
        x = _ffn(l, x3, rows(norm_ffn_pre), wup_b, ffn_conv_w, rows(ffn_conv_b), wdown_b,
                 rows(norm_ffn_post))
    return x
```

```python
import functools

import jax
import jax.numpy as jnp
from jax import lax
from jax.experimental import pallas as pl
from jax.experimental.pallas import tpu as pltpu

F32 = jnp.float32
BF16 = jnp.bfloat16

D_MODEL = 1024
CHUNK = 64
EPS = 1e-6
SB_HEAD_DIM = 64
SB_WIDTH = 512
LRU_WIDTH = 256
LRU_BLOCKS = 4
LRU_CONV = 4
LRU_C = 8.0
SG_WIDTH = 256
SG_GROUPS = 4
SG_GROUP_DIM = 64
SG_CHUNK = 128
IN_WIDTH = 2560
QKV_WIDTH = 3 * SB_WIDTH
XA_HEADS = 4
XA_HEAD_DIM = 256
D_FF = 2816
FFN_CONV = 3

SUBLANES = 8
LANES = 128
VMEM_LIMIT_BYTES = 56 * 1024 * 1024

ROW_TILE = 512
XA_TILE = 1024
FFN_TILE = 1024
SB_BLOCK = 128
SB_QUERY_BLOCKS = 4
FF_CHUNK = 256
FFN_Z_SLOTS = 4
SB_DEAD_LOG = -110.0
LOG2_E = 1.4426950408889634
SB_MASKED_SCORE = -1e30
SB_STATIC_BLOCKS = 3


def _params(*semantics):
    return pltpu.CompilerParams(dimension_semantics=semantics, vmem_limit_bytes=VMEM_LIMIT_BYTES)


def _layer_spec(l, shape):
    index = (l,) + (0,) * len(shape)
    return pl.BlockSpec((None,) + tuple(shape), lambda *_: index, pipeline_mode=pl.Buffered(1))


def _rms(x):
    return x * lax.rsqrt(jnp.mean(x * x, axis=-1, keepdims=True) + EPS)


def _gelu(x):
    return 0.5 * x * (1.0 + jnp.tanh(0.7978845608028654 * (x + 0.044715 * (x * x * x))))


def _sigmoid(x):
    return 1.0 / (1.0 + jnp.exp(-x))


def _in_mix_kernel(x_ref, g_ref, w_ref, cw_ref, cb_ref, wa_ref, ba_ref, wx_ref, bx_ref, lam_ref,
                   sgn_ref, sgw_ref, sgb_ref, qkv_ref, b_ref, c_ref, xpad_ref, h_ref):
    i = pl.program_id(1)
    tm = ROW_TILE
    w = LRU_WIDTH

    @pl.when(i == 0)
    def _():
        h_ref[...] = jnp.zeros_like(h_ref)
        xpad_ref[0:SUBLANES, :] = jnp.zeros((SUBLANES, w), F32)

    h = (_rms(x_ref[...]) * g_ref[...]).astype(BF16)

    def proj(col, width):
        return jnp.dot(h, w_ref[:, col:col + width], preferred_element_type=F32)

    lru_in = proj(QKV_WIDTH, 2 * w)
    sg_in = proj(QKV_WIDTH + 2 * w, 2 * SG_WIDTH)

    xr = lru_in[:, 0:w]
    xpad_ref[SUBLANES:SUBLANES + tm, :] = xr
    xc = cb_ref[...] + cw_ref[LRU_CONV - 1:LRU_CONV, :] * xr
    for k in range(LRU_CONV - 1):
        off = SUBLANES - (LRU_CONV - 1) + k
        xc = xc + cw_ref[k:k + 1, :] * xpad_ref[off:off + tm, :]
    xpad_ref[0:SUBLANES, :] = xr[tm - SUBLANES:tm, :]

    qkv_ref[:, 0:SB_WIDTH] = (proj(0, SB_WIDTH) * (SB_HEAD_DIM ** -0.5)).astype(BF16)

    xcb = xc.astype(BF16)
    r = _sigmoid(jnp.dot(xcb, wa_ref[...], preferred_element_type=F32) + ba_ref[...])
    gate_i = _sigmoid(jnp.dot(xcb, wx_ref[...], preferred_element_type=F32) + bx_ref[...])

    qkv_ref[:, SB_WIDTH:2 * SB_WIDTH] = proj(SB_WIDTH, SB_WIDTH).astype(BF16)
    qkv_ref[:, 2 * SB_WIDTH:3 * SB_WIDTH] = proj(2 * SB_WIDTH, SB_WIDTH).astype(BF16)

    lam = lam_ref[...]
    softplus_neg_lam = jnp.maximum(-lam, 0.0) + jnp.log1p(jnp.exp(-jnp.abs(lam)))
    log_a = (-LRU_C) * r * softplus_neg_lam
    a = jnp.exp(log_a)
    u = jnp.sqrt(-jnp.tanh(log_a) * (a * a + 1.0)) * (gate_i * xc)

    y_gate = _gelu(lru_in[:, w:2 * w])
    slab_row = lax.broadcasted_iota(jnp.int32, (SUBLANES, w), 0)
    h_prev = h_ref[0:1, :]
    for k in range(tm // SUBLANES):
        rows8 = slice(k * SUBLANES, (k + 1) * SUBLANES)
        a8, u8 = a[rows8], u[rows8]
        d = 1
        while d < SUBLANES:
            valid = slab_row >= d
            u8 = a8 * jnp.where(valid, pltpu.roll(u8, d, 0), 0.0) + u8
            a8 = a8 * jnp.where(valid, pltpu.roll(a8, d, 0), 1.0)
            d *= 2
        h8 = u8 + a8 * h_prev
        b_ref[rows8, :] = h8 * y_gate[rows8]
        h_prev = h8[SUBLANES - 1:SUBLANES, :]
    h_ref[...] = jnp.broadcast_to(h_prev, h_ref.shape)

    su = _gelu(sg_in[:, 0:SG_WIDTH])
    sv = _gelu(sg_in[:, SG_WIDTH:2 * SG_WIDTH])
    vn = (_rms(sv) * sgn_ref[...]).astype(BF16)
    pi = lax.broadcasted_iota(jnp.int32, (SG_CHUNK, SG_CHUNK), 0)
    pj = lax.broadcasted_iota(jnp.int32, (SG_CHUNK, SG_CHUNK), 1)
    chunk_causal = (pj // CHUNK) <= (pi // CHUNK)
    lane = lax.broadcasted_iota(jnp.int32, (SG_CHUNK, SG_WIDTH), 1)
    wm = jnp.concatenate([jnp.where(chunk_causal, sgw_ref[g], 0.0).astype(BF16) for g in range(SG_GROUPS)],
                         axis=0)
    for n in range(tm // SG_CHUNK):
        chunk = slice(n * SG_CHUNK, (n + 1) * SG_CHUNK)
        mg = jnp.dot(wm, vn[chunk, :], preferred_element_type=F32)
        mixed = mg[0:SG_CHUNK]
        for g in range(1, SG_GROUPS):
            mixed = jnp.where(lane >= g * SG_GROUP_DIM, mg[g * SG_CHUNK:(g + 1) * SG_CHUNK], mixed)
        c_ref[chunk, :] = su[chunk, :] * (mixed + sgb_ref[...])


def _in_mix(l, x3, g, w_bf16, cw, cb, wa_bd, ba, wx_bd, bx, lam, sgn, sgw, sgb_full):
    b, s, _ = x3.shape
    tm = ROW_TILE
    w = LRU_WIDTH
    rows = lambda width: pl.BlockSpec((None, tm, width), lambda bi, i: (bi, i, 0))
    return pl.pallas_call(
        _in_mix_kernel,
        grid=(b, s // tm),
        in_specs=[
            rows(D_MODEL),
            _layer_spec(l, (1, D_MODEL)),
            _layer_spec(l, (D_MODEL, IN_WIDTH)),
            _layer_spec(l, (LRU_CONV, w)),
            _layer_spec(l, (1, w)),
            _layer_spec(l, (w, w)),
            _layer_spec(l, (1, w)),
            _layer_spec(l, (w, w)),
            _layer_spec(l, (1, w)),
            _layer_spec(l, (1, w)),
            _layer_spec(l, (1, SG_WIDTH)),
            _layer_spec(l, (SG_GROUPS, SG_CHUNK, SG_CHUNK)),
            _layer_spec(l, (SG_CHUNK, SG_WIDTH)),
        ],
        out_specs=[rows(QKV_WIDTH), rows(w), rows(SG_WIDTH)],
        out_shape=[
            jax.ShapeDtypeStruct((b, s, QKV_WIDTH), BF16),
            jax.ShapeDtypeStruct((b, s, w), F32),
            jax.ShapeDtypeStruct((b, s, SG_WIDTH), F32),
        ],
        scratch_shapes=[
            pltpu.VMEM((tm + SUBLANES, w), F32),
            pltpu.VMEM((SUBLANES, w), F32),
        ],
        compiler_params=_params("parallel", "arbitrary"),
        name="in_mix",
    )(x3, g, w_bf16, cw, cb, wa_bd, ba, wx_bd, bx, lam, sgn, sgw, sgb_full)


def _sb_attn_kernel(q_ref, k_ref, v_ref, o_ref):
    blk = SB_BLOCK
    pairs = SB_WIDTH // LANES
    row = lax.broadcasted_iota(jnp.int32, (blk, 2 * blk), 0)
    col = lax.broadcasted_iota(jnp.int32, (blk, 2 * blk), 1)
    causal = (col % blk) < row
    trow = lax.broadcasted_iota(jnp.int32, (2 * blk, 2 * blk), 0)
    tcol = lax.broadcasted_iota(jnp.int32, (2 * blk, 2 * blk), 1)
    suffix_neg_ones = -jnp.logical_and(trow // blk == tcol // blk, trow >= tcol).astype(BF16)
    head0 = lax.broadcasted_iota(jnp.int32, (blk, LANES), 1) < SB_HEAD_DIM

    def split_heads(t):
        zero = jnp.zeros_like(t)
        return jnp.concatenate([jnp.where(head0, t, zero), jnp.where(head0, zero, t)], axis=0)

    def key_rows(j):
        return pl.ds(pl.multiple_of(j * blk, blk), blk)

    def scores(q_rows, p, j, allowed):
        cols = slice(p * LANES, (p + 1) * LANES)
        kk = split_heads(k_ref[key_rows(j), cols])
        s = lax.dot_general(q_ref[q_rows, cols], kk, (((1,), (1,)), ((), ())), preferred_element_type=F32)
        if allowed is not None:
            s = jnp.where(allowed, s, SB_MASKED_SCORE)
        return s

    def suffix_sum(s):
        neg_ls = jnp.maximum(s, 0.0) + jnp.log(1.0 + jnp.exp2(jnp.abs(s) * (-LOG2_E)))
        return jnp.dot(neg_ls.astype(BF16), suffix_neg_ones, preferred_element_type=F32)

    def weighted_values(p, j, s, incl, carry):
        cols = slice(p * LANES, (p + 1) * LANES)
        vv = split_heads(v_ref[key_rows(j), cols])
        c0, c1 = carry
        carry_b = jnp.concatenate([jnp.broadcast_to(c0, (blk, blk)), jnp.broadcast_to(c1, (blk, blk))], axis=1)
        w = jnp.exp2((s + incl + carry_b) * LOG2_E)
        return jnp.dot(w.astype(BF16), vv, preferred_element_type=F32)

    def key_blocks(q_rows, blocks, carries, accs):
        tiles = [(j, p, allowed) for j, allowed in blocks for p in range(pairs)]
        s_all = [scores(q_rows, p, j, allowed) for j, p, allowed in tiles]
        incl_all = [suffix_sum(s) for s in s_all]
        carries, accs = list(carries), list(accs)
        for (j, p, _), s, incl in zip(tiles, s_all, incl_all):
            accs[p] = accs[p] + weighted_values(p, j, s, incl, carries[p])
            carries[p] = (carries[p][0] + incl[:, 0:1], carries[p][1] + incl[:, blk:blk + 1])
        return tuple(carries), tuple(accs)

    def query_block(r, unused):
        i = pl.program_id(1) * SB_QUERY_BLOCKS + r
        q_rows = pl.ds(pl.multiple_of(r * blk, blk), blk)
        zero_col = jnp.zeros((blk, 1), F32)
        carries = tuple((zero_col, zero_col) for _ in range(pairs))
        accs = tuple(jnp.zeros((blk, LANES), F32) for _ in range(pairs))
        static_blocks = [(i, causal)] + [(jnp.maximum(i - d, 0), i - d >= 0)
                                         for d in range(1, SB_STATIC_BLOCKS)]
        carries, accs = key_blocks(q_rows, static_blocks, carries, accs)

        def cond(state):
            j, carries, _ = state
            live = functools.reduce(jnp.maximum, [c for pair in carries for c in pair])
            return jnp.logical_and(j >= 0, jnp.max(live) > SB_DEAD_LOG)

        def body(state):
            j, carries, accs = state
            carries, accs = key_blocks(q_rows, [(j, None)], carries, accs)
            return j - 1, carries, accs

        _, _, accs = lax.while_loop(cond, body, (i - SB_STATIC_BLOCKS, carries, accs))
        for p in range(pairs):
            o_ref[q_rows, p * LANES:(p + 1) * LANES] = accs[p]
        return unused

    lax.fori_loop(0, SB_QUERY_BLOCKS, query_block, 0)


def _sb_attention(qkv3):
    b, s, _ = qkv3.shape
    q_tile = SB_QUERY_BLOCKS * SB_BLOCK
    return pl.pallas_call(
        _sb_attn_kernel,
        grid=(b, s // q_tile),
        in_specs=[
            pl.BlockSpec((None, q_tile, SB_WIDTH), lambda bi, i: (bi, i, 0)),
            pl.BlockSpec((None, s, SB_WIDTH), lambda bi, i: (bi, 0, 1)),
            pl.BlockSpec((None, s, SB_WIDTH), lambda bi, i: (bi, 0, 2)),
        ],
        out_specs=pl.BlockSpec((None, q_tile, SB_WIDTH), lambda bi, i: (bi, i, 0)),
        out_shape=jax.ShapeDtypeStruct((b, s, SB_WIDTH), F32),
        compiler_params=_params("parallel", "parallel"),
        name="sb_attn",
    )(qkv3, qkv3, qkv3)


def _mem_kv_kernel(mem_ref, g_ref, w_ref, kv_ref):
    mn = (_rms(mem_ref[...]) * g_ref[...]).astype(BF16)
    kv_ref[...] = jnp.dot(mn, w_ref[...], preferred_element_type=F32).astype(BF16)


def _mem_kv(l, mem, g, w_bf16):
    b, m, _ = mem.shape
    return pl.pallas_call(
        _mem_kv_kernel,
        grid=(b,),
        in_specs=[
            pl.BlockSpec((None, m, D_MODEL), lambda bi: (bi, 0, 0)),
            _layer_spec(l, (1, D_MODEL)),
            _layer_spec(l, (D_MODEL, 2 * D_MODEL)),
        ],
        out_specs=pl.BlockSpec((None, m, 2 * D_MODEL), lambda bi: (bi, 0, 0)),
        out_shape=jax.ShapeDtypeStruct((b, m, 2 * D_MODEL), BF16),
        compiler_params=_params("parallel"),
        name="mem_kv",
    )(mem, g, w_bf16)


def _mix_out_xattn_kernel(x_ref, a_ref, b_ref, c_ref, mn_ref, wout_ref, gmix_ref, gpre_ref, wq_ref,
                          kv_ref, wo_ref, gpost_ref, o_ref):
    groups = [slice(g * ROW_TILE, (g + 1) * ROW_TILE) for g in range(XA_TILE // ROW_TILE)]
    heads = [slice(hd * XA_HEAD_DIM, (hd + 1) * XA_HEAD_DIM) for hd in range(XA_HEADS)]
    nt = (((1,), (1,)), ((), ()))

    mixed = [(jnp.concatenate([_rms(a_ref[r, :]), _rms(b_ref[r, :]), _rms(c_ref[r, :])], axis=-1)
              * mn_ref[...]).astype(BF16) for r in groups]
    y = [jnp.dot(m, wout_ref[...], preferred_element_type=F32) for m in mixed]
    x1 = [x_ref[r, :] + _rms(yg) * gmix_ref[...] for r, yg in zip(groups, y)]
    h = [(_rms(xg) * gpre_ref[...]).astype(BF16) for xg in x1]
    q = [(jnp.dot(hg, wq_ref[...], preferred_element_type=F32) * (XA_HEAD_DIM ** -0.5)).astype(BF16)
         for hg in h]
    s = [[lax.dot_general(qg[:, hs], kv_ref[:, hs], nt, preferred_element_type=F32) for hs in heads]
         for qg in q]
    e = [[jnp.exp(sh - jnp.max(sh, axis=-1, keepdims=True)) for sh in sg] for sg in s]
    p = [[(eh / jnp.sum(eh, axis=-1, keepdims=True)).astype(BF16) for eh in eg] for eg in e]
    o = [jnp.concatenate(
        [jnp.dot(ph, kv_ref[:, D_MODEL + hs.start:D_MODEL + hs.stop], preferred_element_type=F32).astype(BF16)
         for ph, hs in zip(pg, heads)], axis=-1) for pg in p]
    y2 = [jnp.dot(og, wo_ref[...], preferred_element_type=F32) for og in o]
    for r, xg, yg in zip(groups, x1, y2):
        o_ref[r, :] = xg + _rms(yg) * gpost_ref[...]


def _mix_out_xattn(l, x3, a3, b3, c3, mn, wout_bf16, gmix, gpre, wq_bf16, kv, wo_bf16, gpost):
    b, s, _ = x3.shape
    m = kv.shape[1]
    tm = XA_TILE
    rows = lambda width: pl.BlockSpec((None, tm, width), lambda bi, i: (bi, i, 0))
    return pl.pallas_call(
        _mix_out_xattn_kernel,
        grid=(b, s // tm),
        in_specs=[
            rows(D_MODEL), rows(SB_WIDTH), rows(LRU_WIDTH), rows(SG_WIDTH),
            _layer_spec(l, (1, D_MODEL)),
            _layer_spec(l, (D_MODEL, D_MODEL)),
            _layer_spec(l, (1, D_MODEL)),
            _layer_spec(l, (1, D_MODEL)),
            _layer_spec(l, (D_MODEL, D_MODEL)),
            pl.BlockSpec((None, m, 2 * D_MODEL), lambda bi, i: (bi, 0, 0)),
            _layer_spec(l, (D_MODEL, D_MODEL)),
            _layer_spec(l, (1, D_MODEL)),
        ],
        out_specs=rows(D_MODEL),
        out_shape=jax.ShapeDtypeStruct((b, s, D_MODEL), F32),
        compiler_params=_params("parallel", "parallel"),
        name="mix_out_xattn",
    )(x3, a3, b3, c3, mn, wout_bf16, gmix, gpre, wq_bf16, kv, wo_bf16, gpost)


def _ffn_kernel(x_ref, halo_ref, gpre_ref, wup_ref, cw_ref, cb_ref, wdown_ref, gpost_ref, o_ref,
                z_ref, act_ref):
    i = pl.program_id(1)
    tm = ROW_TILE
    n_groups = FFN_TILE // tm

    def up_proj(h, col):
        return jnp.dot(h, wup_ref[:, col:col + FF_CHUNK], preferred_element_type=F32)

    def conv(g, z, col, slot):
        halves = []
        for hf in range(FF_CHUNK // LANES):
            lanes = slice(hf * LANES, (hf + 1) * LANES)
            wcol = slice(col + hf * LANES, col + (hf + 1) * LANES)
            z_halo = jnp.where(i == 0, 0.0, z[0:SUBLANES, lanes]) if g == 0 else z[0:SUBLANES, lanes]
            z_ref[g, slot, hf, 0:SUBLANES, :] = z_halo
            z_ref[g, slot, hf, SUBLANES:, :] = z[SUBLANES:, lanes]
            out = cb_ref[:, wcol] + cw_ref[FFN_CONV - 1:FFN_CONV, wcol] * z[SUBLANES:, lanes]
            for k in range(FFN_CONV - 1):
                off = SUBLANES - (FFN_CONV - 1) + k
                out = out + cw_ref[k:k + 1, wcol] * z_ref[g, slot, hf, off:off + tm, :]
            halves.append(out)
        return jnp.concatenate(halves, axis=-1)

    xs = [x_ref[g * tm:(g + 1) * tm, :] for g in range(n_groups)]
    halos = [halo_ref[...]] + [x_ref[g * tm - SUBLANES:g * tm, :] for g in range(1, n_groups)]
    hs = [(_rms(jnp.concatenate([halo, x], axis=0)) * gpre_ref[...]).astype(BF16)
          for halo, x in zip(halos, xs)]
    for g in range(n_groups):
        for c in range(D_FF // FF_CHUNK):
            gate = conv(g, up_proj(hs[g], c * FF_CHUNK), c * FF_CHUNK, (2 * c) % FFN_Z_SLOTS)
            up = conv(g, up_proj(hs[g], D_FF + c * FF_CHUNK), D_FF + c * FF_CHUNK, (2 * c + 1) % FFN_Z_SLOTS)
            act_ref[g, :, c * FF_CHUNK:(c + 1) * FF_CHUNK] = (_gelu(gate) * up).astype(BF16)
    ys = [jnp.dot(act_ref[g], wdown_ref[...], preferred_element_type=F32) for g in range(n_groups)]
    for g in range(n_groups):
        o_ref[g * tm:(g + 1) * tm, :] = xs[g] + _rms(ys[g]) * gpost_ref[...]


def _ffn(l, x3, gpre, wup_bf16, cw, cb, wdown_bf16, gpost):
    b, s, _ = x3.shape
    tm = FFN_TILE
    groups = FFN_TILE // ROW_TILE
    halo_blocks = tm // SUBLANES
    return pl.pallas_call(
        _ffn_kernel,
        grid=(b, s // tm),
        in_specs=[
            pl.BlockSpec((None, tm, D_MODEL), lambda bi, i: (bi, i, 0)),
            pl.BlockSpec((None, SUBLANES, D_MODEL),
                         lambda bi, i: (bi, jnp.maximum(i * halo_blocks - 1, 0), 0)),
            _layer_spec(l, (1, D_MODEL)),
            _layer_spec(l, (D_MODEL, 2 * D_FF)),
            _layer_spec(l, (FFN_CONV, 2 * D_FF)),
            _layer_spec(l, (1, 2 * D_FF)),
            _layer_spec(l, (D_FF, D_MODEL)),
            _layer_spec(l, (1, D_MODEL)),
        ],
        out_specs=pl.BlockSpec((None, tm, D_MODEL), lambda bi, i: (bi, i, 0)),
        out_shape=jax.ShapeDtypeStruct((b, s, D_MODEL), F32),
        scratch_shapes=[
            pltpu.VMEM((groups, FFN_Z_SLOTS, FF_CHUNK // LANES, ROW_TILE + SUBLANES, LANES), F32),
            pltpu.VMEM((groups, ROW_TILE, D_FF), BF16)],
        compiler_params=_params("parallel", "parallel"),
        name="ffn",
    )(x3, x3, gpre, wup_bf16, cw, cb, wdown_bf16, gpost)


def _block_diag(w):
    layers, n, c, d = w.shape
    eye = jnp.eye(n, dtype=w.dtype)
    return (eye[None, :, None, :, None] * w[:, :, :, None, :]).reshape(layers, n * c, n * d)


def kernel(x, mem, norm_mix_pre, w_in, lru_conv_w, lru_conv_b, lru_w_a, lru_b_a, lru_w_x, lru_b_x, lru_lambda, sg_norm, sg_w, sg_b, mix_norm, w_out, norm_mix_post, norm_xa_pre, norm_mem, xa_w_q, xa_w_kv, xa_w_o, norm_xa_post, norm_ffn_pre, ffn_w_up, ffn_conv_w, ffn_conv_b, ffn_w_down, norm_ffn_post):
    depth = w_in.shape[0]
    rows = lambda v: v.reshape(depth, 1, -1)
    w_in_b, w_out_b = w_in.astype(BF16), w_out.astype(BF16)
    wq_b, wkv_b, wo_b = xa_w_q.astype(BF16), xa_w_kv.astype(BF16), xa_w_o.astype(BF16)
    wup_b, wdown_b = ffn_w_up.astype(BF16), ffn_w_down.astype(BF16)
    wa_bd, wx_bd = _block_diag(lru_w_a).astype(BF16), _block_diag(lru_w_x).astype(BF16)
    sgb_full = jnp.repeat(jnp.swapaxes(sg_b, 1, 2), SG_GROUP_DIM, axis=2)

    for l in range(depth):
        qkv, b_out, c_out = _in_mix(
            l, x, rows(norm_mix_pre), w_in_b, lru_conv_w, rows(lru_conv_b), wa_bd, rows(lru_b_a),
            wx_bd, rows(lru_b_x), rows(lru_lambda), rows(sg_norm), sg_w, sgb_full)
        a_out = _sb_attention(qkv)
        kv = _mem_kv(l, mem, rows(norm_mem), wkv_b)
        x3 = _mix_out_xattn(l, x, a_out, b_out, c_out, rows(mix_norm), w_out_b, rows(norm_mix_post),
                            rows(norm_xa_pre), wq_b, kv, wo_b, rows(norm_xa_post))
        x = _ffn(l, x3, rows(norm_ffn_pre), wup_b, ffn_conv_w, rows(ffn_conv_b), wdown_b,
                 rows(norm_ffn_post))
    return x
```

```python
import functools

import jax
import jax.numpy as jnp
from jax import lax
from jax.experimental import pallas as pl
from jax.experimental.pallas import tpu as pltpu

F32 = jnp.float32
BF16 = jnp.bfloat16

D_MODEL = 1024
CHUNK = 64
EPS = 1e-6
SB_HEAD_DIM = 64
SB_WIDTH = 512
LRU_WIDTH = 256
LRU_BLOCKS = 4
LRU_CONV = 4
LRU_C = 8.0
SG_WIDTH = 256
SG_GROUPS = 4
SG_GROUP_DIM = 64
SG_CHUNK = 128
IN_WIDTH = 2560
QKV_WIDTH = 3 * SB_WIDTH
XA_HEADS = 4
XA_HEAD_DIM = 256
D_FF = 2816
FFN_CONV = 3

SUBLANES = 8
LANES = 128
VMEM_LIMIT_BYTES = 56 * 1024 * 1024

ROW_TILE = 512
XA_TILE = 1024
FFN_TILE = 1024
SB_BLOCK = 128
SB_QUERY_BLOCKS = 4
FF_CHUNK = 256
FFN_Z_SLOTS = 4
SB_DEAD_LOG = -110.0
LOG2_E = 1.4426950408889634
SB_MASKED_SCORE = -1e30
SB_STATIC_BLOCKS = 3


def _params(*semantics):
    return pltpu.CompilerParams(dimension_semantics=semantics, vmem_limit_bytes=VMEM_LIMIT_BYTES)


def _layer_spec(l, shape):
    index = (l,) + (0,) * len(shape)
    return pl.BlockSpec((None,) + tuple(shape), lambda *_: index, pipeline_mode=pl.Buffered(1))


def _rms(x):
    return x * lax.rsqrt(jnp.mean(x * x, axis=-1, keepdims=True) + EPS)


def _gelu(x):
    return 0.5 * x * (1.0 + jnp.tanh(0.7978845608028654 * (x + 0.044715 * (x * x * x))))


def _sigmoid(x):
    return 1.0 / (1.0 + jnp.exp(-x))


def _in_mix_kernel(x_ref, g_ref, w_ref, cw_ref, cb_ref, wa_ref, ba_ref, wx_ref, bx_ref, lam_ref,
                   sgn_ref, sgw_ref, sgb_ref, qkv_ref, b_ref, c_ref, xpad_ref, h_ref):
    i = pl.program_id(1)
    tm = ROW_TILE
    w = LRU_WIDTH

    @pl.when(i == 0)
    def _():
        h_ref[...] = jnp.zeros_like(h_ref)
        xpad_ref[0:SUBLANES, :] = jnp.zeros((SUBLANES, w), F32)

    h = (_rms(x_ref[...]) * g_ref[...]).astype(BF16)

    def proj(col, width):
        return jnp.dot(h, w_ref[:, col:col + width], preferred_element_type=F32)

    lru_in = proj(QKV_WIDTH, 2 * w)
    sg_in = proj(QKV_WIDTH + 2 * w, 2 * SG_WIDTH)

    xr = lru_in[:, 0:w]
    xpad_ref[SUBLANES:SUBLANES + tm, :] = xr
    xc = cb_ref[...] + cw_ref[LRU_CONV - 1:LRU_CONV, :] * xr
    for k in range(LRU_CONV - 1):
        off = SUBLANES - (LRU_CONV - 1) + k
        xc = xc + cw_ref[k:k + 1, :] * xpad_ref[off:off + tm, :]
    xpad_ref[0:SUBLANES, :] = xr[tm - SUBLANES:tm, :]

    qkv_ref[:, 0:SB_WIDTH] = (proj(0, SB_WIDTH) * (SB_HEAD_DIM ** -0.5)).astype(BF16)

    xcb = xc.astype(BF16)
    r = _sigmoid(jnp.dot(xcb, wa_ref[...], preferred_element_type=F32) + ba_ref[...])
    gate_i = _sigmoid(jnp.dot(xcb, wx_ref[...], preferred_element_type=F32) + bx_ref[...])

    qkv_ref[:, SB_WIDTH:2 * SB_WIDTH] = proj(SB_WIDTH, SB_WIDTH).astype(BF16)
    qkv_ref[:, 2 * SB_WIDTH:3 * SB_WIDTH] = proj(2 * SB_WIDTH, SB_WIDTH).astype(BF16)

    lam = lam_ref[...]
    softplus_neg_lam = jnp.maximum(-lam, 0.0) + jnp.log1p(jnp.exp(-jnp.abs(lam)))
    log_a = (-LRU_C) * r * softplus_neg_lam
    a = jnp.exp(log_a)
    u = jnp.sqrt(-jnp.tanh(log_a) * (a * a + 1.0)) * (gate_i * xc)

    y_gate = _gelu(lru_in[:, w:2 * w])
    slab_row = lax.broadcasted_iota(jnp.int32, (SUBLANES, w), 0)
    h_prev = h_ref[0:1, :]
    for k in range(tm // SUBLANES):
        rows8 = slice(k * SUBLANES, (k + 1) * SUBLANES)
        a8, u8 = a[rows8], u[rows8]
        d = 1
        while d < SUBLANES:
            valid = slab_row >= d
            u8 = a8 * jnp.where(valid, pltpu.roll(u8, d, 0), 0.0) + u8
            a8 = a8 * jnp.where(valid, pltpu.roll(a8, d, 0), 1.0)
            d *= 2
        h8 = u8 + a8 * h_prev
        b_ref[rows8, :] = h8 * y_gate[rows8]
        h_prev = h8[SUBLANES - 1:SUBLANES, :]
    h_ref[...] = jnp.broadcast_to(h_prev, h_ref.shape)

    su = _gelu(sg_in[:, 0:SG_WIDTH])
    sv = _gelu(sg_in[:, SG_WIDTH:2 * SG_WIDTH])
    vn = (_rms(sv) * sgn_ref[...]).astype(BF16)
    pi = lax.broadcasted_iota(jnp.int32, (SG_CHUNK, SG_CHUNK), 0)
    pj = lax.broadcasted_iota(jnp.int32, (SG_CHUNK, SG_CHUNK), 1)
    chunk_causal = (pj // CHUNK) <= (pi // CHUNK)
    lane = lax.broadcasted_iota(jnp.int32, (SG_CHUNK, SG_WIDTH), 1)
    wm = jnp.concatenate([jnp.where(chunk_causal, sgw_ref[g], 0.0).astype(BF16) for g in range(SG_GROUPS)],
                         axis=0)
    for n in range(tm // SG_CHUNK):
        chunk = slice(n * SG_CHUNK, (n + 1) * SG_CHUNK)
        mg = jnp.dot(wm, vn[chunk, :], preferred_element_type=F32)
        mixed = mg[0:SG_CHUNK]
        for g in range(1, SG_GROUPS):
            mixed = jnp.where(lane >= g * SG_GROUP_DIM, mg[g * SG_CHUNK:(g + 1) * SG_CHUNK], mixed)
        c_ref[chunk, :] = su[chunk, :] * (mixed + sgb_ref[...])


def _in_mix(l, x3, g, w_bf16, cw, cb, wa_bd, ba, wx_bd, bx, lam, sgn, sgw, sgb_full):
    b, s, _ = x3.shape
    tm = ROW_TILE
    w = LRU_WIDTH
    rows = lambda width: pl.BlockSpec((None, tm, width), lambda bi, i: (bi, i, 0))
    return pl.pallas_call(
        _in_mix_kernel,
        grid=(b, s // tm),
        in_specs=[
            rows(D_MODEL),
            _layer_spec(l, (1, D_MODEL)),
            _layer_spec(l, (D_MODEL, IN_WIDTH)),
            _layer_spec(l, (LRU_CONV, w)),
            _layer_spec(l, (1, w)),
            _layer_spec(l, (w, w)),
            _layer_spec(l, (1, w)),
            _layer_spec(l, (w, w)),
            _layer_spec(l, (1, w)),
            _layer_spec(l, (1, w)),
            _layer_spec(l, (1, SG_WIDTH)),
            _layer_spec(l, (SG_GROUPS, SG_CHUNK, SG_CHUNK)),
            _layer_spec(l, (SG_CHUNK, SG_WIDTH)),
        ],
        out_specs=[rows(QKV_WIDTH), rows(w), rows(SG_WIDTH)],
        out_shape=[
            jax.ShapeDtypeStruct((b, s, QKV_WIDTH), BF16),
            jax.ShapeDtypeStruct((b, s, w), F32),
            jax.ShapeDtypeStruct((b, s, SG_WIDTH), F32),
        ],
        scratch_shapes=[
            pltpu.VMEM((tm + SUBLANES, w), F32),
            pltpu.VMEM((SUBLANES, w), F32),
        ],
        compiler_params=_params("parallel", "arbitrary"),
        name="in_mix",
    )(x3, g, w_bf16, cw, cb, wa_bd, ba, wx_bd, bx, lam, sgn, sgw, sgb_full)


def _sb_attn_kernel(q_ref, k_ref, v_ref, o_ref):
    blk = SB_BLOCK
    pairs = SB_WIDTH // LANES
    row = lax.broadcasted_iota(jnp.int32, (blk, 2 * blk), 0)
    col = lax.broadcasted_iota(jnp.int32, (blk, 2 * blk), 1)
    causal = (col % blk) < row
    trow = lax.broadcasted_iota(jnp.int32, (2 * blk, 2 * blk), 0)
    tcol = lax.broadcasted_iota(jnp.int32, (2 * blk, 2 * blk), 1)
    suffix_neg_ones = -jnp.logical_and(trow // blk == tcol // blk, trow >= tcol).astype(BF16)
    head0 = lax.broadcasted_iota(jnp.int32, (blk, LANES), 1) < SB_HEAD_DIM

    def split_heads(t):
        zero = jnp.zeros_like(t)
        return jnp.concatenate([jnp.where(head0, t, zero), jnp.where(head0, zero, t)], axis=0)

    def key_rows(j):
        return pl.ds(pl.multiple_of(j * blk, blk), blk)

    def scores(q_rows, p, j, allowed):
        cols = slice(p * LANES, (p + 1) * LANES)
        kk = split_heads(k_ref[key_rows(j), cols])
        s = lax.dot_general(q_ref[q_rows, cols], kk, (((1,), (1,)), ((), ())), preferred_element_type=F32)
        if allowed is not None:
            s = jnp.where(allowed, s, SB_MASKED_SCORE)
        return s

    def suffix_sum(s):
        sb = s.astype(BF16)
        neg_ls = jnp.maximum(sb, 0.0) + jnp.log(1.0 + jnp.exp2(jnp.abs(sb) * (-LOG2_E)))
        return jnp.dot(neg_ls.astype(BF16), suffix_neg_ones, preferred_element_type=F32)

    def weighted_values(p, j, s, incl, carry):
        cols = slice(p * LANES, (p + 1) * LANES)
        vv = split_heads(v_ref[key_rows(j), cols])
        c0, c1 = carry
        carry_b = jnp.concatenate([jnp.broadcast_to(c0, (blk, blk)), jnp.broadcast_to(c1, (blk, blk))], axis=1)
        w = jnp.exp2((s + incl + carry_b) * LOG2_E)
        return jnp.dot(w.astype(BF16), vv, preferred_element_type=F32)

    def key_blocks(q_rows, blocks, carries, accs):
        tiles = [(j, p, allowed) for j, allowed in blocks for p in range(pairs)]
        s_all = [scores(q_rows, p, j, allowed) for j, p, allowed in tiles]
        incl_all = [suffix_sum(s) for s in s_all]
        carries, accs = list(carries), list(accs)
        for (j, p, _), s, incl in zip(tiles, s_all, incl_all):
            accs[p] = accs[p] + weighted_values(p, j, s, incl, carries[p])
            carries[p] = (carries[p][0] + incl[:, 0:1], carries[p][1] + incl[:, blk:blk + 1])
        return tuple(carries), tuple(accs)

    def query_block(r, unused):
        i = pl.program_id(1) * SB_QUERY_BLOCKS + r
        q_rows = pl.ds(pl.multiple_of(r * blk, blk), blk)
        zero_col = jnp.zeros((blk, 1), F32)
        carries = tuple((zero_col, zero_col) for _ in range(pairs))
        accs = tuple(jnp.zeros((blk, LANES), F32) for _ in range(pairs))
        static_blocks = [(i, causal)] + [(jnp.maximum(i - d, 0), i - d >= 0)
                                         for d in range(1, SB_STATIC_BLOCKS)]
        carries, accs = key_blocks(q_rows, static_blocks, carries, accs)

        def cond(state):
            j, carries, _ = state
            live = functools.reduce(jnp.maximum, [c for pair in carries for c in pair])
            return jnp.logical_and(j >= 0, jnp.max(live) > SB_DEAD_LOG)

        def body(state):
            j, carries, accs = state
            carries, accs = key_blocks(q_rows, [(j, None)], carries, accs)
            return j - 1, carries, accs

        _, _, accs = lax.while_loop(cond, body, (i - SB_STATIC_BLOCKS, carries, accs))
        for p in range(pairs):
            o_ref[q_rows, p * LANES:(p + 1) * LANES] = accs[p]
        return unused

    lax.fori_loop(0, SB_QUERY_BLOCKS, query_block, 0)


def _sb_attention(qkv3):
    b, s, _ = qkv3.shape
    q_tile = SB_QUERY_BLOCKS * SB_BLOCK
    return pl.pallas_call(
        _sb_attn_kernel,
        grid=(b, s // q_tile),
        in_specs=[
            pl.BlockSpec((None, q_tile, SB_WIDTH), lambda bi, i: (bi, i, 0)),
            pl.BlockSpec((None, s, SB_WIDTH), lambda bi, i: (bi, 0, 1)),
            pl.BlockSpec((None, s, SB_WIDTH), lambda bi, i: (bi, 0, 2)),
        ],
        out_specs=pl.BlockSpec((None, q_tile, SB_WIDTH), lambda bi, i: (bi, i, 0)),
        out_shape=jax.ShapeDtypeStruct((b, s, SB_WIDTH), F32),
        compiler_params=_params("parallel", "parallel"),
        name="sb_attn",
    )(qkv3, qkv3, qkv3)


def _mem_kv_kernel(mem_ref, g_ref, w_ref, kv_ref):
    mn = (_rms(mem_ref[...]) * g_ref[...]).astype(BF16)
    kv_ref[...] = jnp.dot(mn, w_ref[...], preferred_element_type=F32).astype(BF16)


def _mem_kv(l, mem, g, w_bf16):
    b, m, _ = mem.shape
    return pl.pallas_call(
        _mem_kv_kernel,
        grid=(b,),
        in_specs=[
            pl.BlockSpec((None, m, D_MODEL), lambda bi: (bi, 0, 0)),
            _layer_spec(l, (1, D_MODEL)),
            _layer_spec(l, (D_MODEL, 2 * D_MODEL)),
        ],
        out_specs=pl.BlockSpec((None, m, 2 * D_MODEL), lambda bi: (bi, 0, 0)),
        out_shape=jax.ShapeDtypeStruct((b, m, 2 * D_MODEL), BF16),
        compiler_params=_params("parallel"),
        name="mem_kv",
    )(mem, g, w_bf16)


def _mix_out_xattn_kernel(x_ref, a_ref, b_ref, c_ref, mn_ref, wout_ref, gmix_ref, gpre_ref, wq_ref,
                          kv_ref, wo_ref, gpost_ref, o_ref):
    groups = [slice(g * ROW_TILE, (g + 1) * ROW_TILE) for g in range(XA_TILE // ROW_TILE)]
    heads = [slice(hd * XA_HEAD_DIM, (hd + 1) * XA_HEAD_DIM) for hd in range(XA_HEADS)]
    nt = (((1,), (1,)), ((), ()))

    mixed = [(jnp.concatenate([_rms(a_ref[r, :]), _rms(b_ref[r, :]), _rms(c_ref[r, :])], axis=-1)
              * mn_ref[...]).astype(BF16) for r in groups]
    y = [jnp.dot(m, wout_ref[...], preferred_element_type=F32) for m in mixed]
    x1 = [x_ref[r, :] + _rms(yg) * gmix_ref[...] for r, yg in zip(groups, y)]
    h = [(_rms(xg) * gpre_ref[...]).astype(BF16) for xg in x1]
    q = [(jnp.dot(hg, wq_ref[...], preferred_element_type=F32) * (XA_HEAD_DIM ** -0.5)).astype(BF16)
         for hg in h]
    s = [[lax.dot_general(qg[:, hs], kv_ref[:, hs], nt, preferred_element_type=F32) for hs in heads]
         for qg in q]
    e = [[jnp.exp(sh - jnp.max(sh, axis=-1, keepdims=True)) for sh in sg] for sg in s]
    p = [[(eh / jnp.sum(eh, axis=-1, keepdims=True)).astype(BF16) for eh in eg] for eg in e]
    o = [jnp.concatenate(
        [jnp.dot(ph, kv_ref[:, D_MODEL + hs.start:D_MODEL + hs.stop], preferred_element_type=F32).astype(BF16)
         for ph, hs in zip(pg, heads)], axis=-1) for pg in p]
    y2 = [jnp.dot(og, wo_ref[...], preferred_element_type=F32) for og in o]
    for r, xg, yg in zip(groups, x1, y2):
        o_ref[r, :] = xg + _rms(yg) * gpost_ref[...]


def _mix_out_xattn(l, x3, a3, b3, c3, mn, wout_bf16, gmix, gpre, wq_bf16, kv, wo_bf16, gpost):
    b, s, _ = x3.shape
    m = kv.shape[1]
    tm = XA_TILE
    rows = lambda width: pl.BlockSpec((None, tm, width), lambda bi, i: (bi, i, 0))
    return pl.pallas_call(
        _mix_out_xattn_kernel,
        grid=(b, s // tm),
        in_specs=[
            rows(D_MODEL), rows(SB_WIDTH), rows(LRU_WIDTH), rows(SG_WIDTH),
            _layer_spec(l, (1, D_MODEL)),
            _layer_spec(l, (D_MODEL, D_MODEL)),
            _layer_spec(l, (1, D_MODEL)),
            _layer_spec(l, (1, D_MODEL)),
            _layer_spec(l, (D_MODEL, D_MODEL)),
            pl.BlockSpec((None, m, 2 * D_MODEL), lambda bi, i: (bi, 0, 0)),
            _layer_spec(l, (D_MODEL, D_MODEL)),
            _layer_spec(l, (1, D_MODEL)),
        ],
        out_specs=rows(D_MODEL),
        out_shape=jax.ShapeDtypeStruct((b, s, D_MODEL), F32),
        compiler_params=_params("parallel", "parallel"),
        name="mix_out_xattn",
    )(x3, a3, b3, c3, mn, wout_bf16, gmix, gpre, wq_bf16, kv, wo_bf16, gpost)


def _ffn_kernel(x_ref, halo_ref, gpre_ref, wup_ref, cw_ref, cb_ref, wdown_ref, gpost_ref, o_ref,
                z_ref, act_ref):
    i = pl.program_id(1)
    tm = ROW_TILE
    n_groups = FFN_TILE // tm

    def up_proj(h, col):
        return jnp.dot(h, wup_ref[:, col:col + FF_CHUNK], preferred_element_type=F32)

    def conv(g, z, col, slot):
        halves = []
        for hf in range(FF_CHUNK // LANES):
            lanes = slice(hf * LANES, (hf + 1) * LANES)
            wcol = slice(col + hf * LANES, col + (hf + 1) * LANES)
            z_halo = jnp.where(i == 0, 0.0, z[0:SUBLANES, lanes]) if g == 0 else z[0:SUBLANES, lanes]
            z_ref[g, slot, hf, 0:SUBLANES, :] = z_halo
            z_ref[g, slot, hf, SUBLANES:, :] = z[SUBLANES:, lanes]
            out = cb_ref[:, wcol] + cw_ref[FFN_CONV - 1:FFN_CONV, wcol] * z[SUBLANES:, lanes]
            for k in range(FFN_CONV - 1):
                off = SUBLANES - (FFN_CONV - 1) + k
                out = out + cw_ref[k:k + 1, wcol] * z_ref[g, slot, hf, off:off + tm, :]
            halves.append(out)
        return jnp.concatenate(halves, axis=-1)

    xs = [x_ref[g * tm:(g + 1) * tm, :] for g in range(n_groups)]
    halos = [halo_ref[...]] + [x_ref[g * tm - SUBLANES:g * tm, :] for g in range(1, n_groups)]
    hs = [(_rms(jnp.concatenate([halo, x], axis=0)) * gpre_ref[...]).astype(BF16)
          for halo, x in zip(halos, xs)]
    for g in range(n_groups):
        for c in range(D_FF // FF_CHUNK):
            gate = conv(g, up_proj(hs[g], c * FF_CHUNK), c * FF_CHUNK, (2 * c) % FFN_Z_SLOTS)
            up = conv(g, up_proj(hs[g], D_FF + c * FF_CHUNK), D_FF + c * FF_CHUNK, (2 * c + 1) % FFN_Z_SLOTS)
            act_ref[g, :, c * FF_CHUNK:(c + 1) * FF_CHUNK] = (_gelu(gate) * up).astype(BF16)
    ys = [jnp.dot(act_ref[g], wdown_ref[...], preferred_element_type=F32) for g in range(n_groups)]
    for g in range(n_groups):
        o_ref[g * tm:(g + 1) * tm, :] = xs[g] + _rms(ys[g]) * gpost_ref[...]


def _ffn(l, x3, gpre, wup_bf16, cw, cb, wdown_bf16, gpost):
    b, s, _ = x3.shape
    tm = FFN_TILE
    groups = FFN_TILE // ROW_TILE
    halo_blocks = tm // SUBLANES
    return pl.pallas_call(
        _ffn_kernel,
        grid=(b, s // tm),
        in_specs=[
            pl.BlockSpec((None, tm, D_MODEL), lambda bi, i: (bi, i, 0)),
            pl.BlockSpec((None, SUBLANES, D_MODEL),
                         lambda bi, i: (bi, jnp.maximum(i * halo_blocks - 1, 0), 0)),
            _layer_spec(l, (1, D_MODEL)),
            _layer_spec(l, (D_MODEL, 2 * D_FF)),
            _layer_spec(l, (FFN_CONV, 2 * D_FF)),
            _layer_spec(l, (1, 2 * D_FF)),
            _layer_spec(l, (D_FF, D_MODEL)),
            _layer_spec(l, (1, D_MODEL)),
        ],
        out_specs=pl.BlockSpec((None, tm, D_MODEL), lambda bi, i: (bi, i, 0)),
        out_shape=jax.ShapeDtypeStruct((b, s, D_MODEL), F32),
        scratch_shapes=[
            pltpu.VMEM((groups, FFN_Z_SLOTS, FF_CHUNK // LANES, ROW_TILE + SUBLANES, LANES), F32),
            pltpu.VMEM((groups, ROW_TILE, D_FF), BF16)],
        compiler_params=_params("parallel", "parallel"),
        name="ffn",
    )(x3, x3, gpre, wup_bf16, cw, cb, wdown_bf16, gpost)


def _block_diag(w):
    layers, n, c, d = w.shape
    eye = jnp.eye(n, dtype=w.dtype)
    return (eye[None, :, None, :, None] * w[:, :, :, None, :]).reshape(layers, n * c, n * d)


def kernel(x, mem, norm_mix_pre, w_in, lru_conv_w, lru_conv_b, lru_w_a, lru_b_a, lru_w_x, lru_b_x, lru_lambda, sg_norm, sg_w, sg_b, mix_norm, w_out, norm_mix_post, norm_xa_pre, norm_mem, xa_w_q, xa_w_kv, xa_w_o, norm_xa_post, norm_ffn_pre, ffn_w_up, ffn_conv_w, ffn_conv_b, ffn_w_down, norm_ffn_post):
    depth = w_in.shape[0]
    rows = lambda v: v.reshape(depth, 1, -1)
    w_in_b, w_out_b = w_in.astype(BF16), w_out.astype(BF16)
    wq_b, wkv_b, wo_b = xa_w_q.astype(BF16), xa_w_kv.astype(BF16), xa_w_o.astype(BF16)
    wup_b, wdown_b = ffn_w_up.astype(BF16), ffn_w_down.astype(BF16)
    wa_bd, wx_bd = _block_diag(lru_w_a).astype(BF16), _block_diag(lru_w_x).astype(BF16)
    sgb_full = jnp.repeat(jnp.swapaxes(sg_b, 1, 2), SG_GROUP_DIM, axis=2)

    for l in range(depth):
        qkv, b_out, c_out = _in_mix(
            l, x, rows(norm_mix_pre), w_in_b, lru_conv_w, rows(lru_conv_b), wa_bd, rows(lru_b_a),
            wx_bd, rows(lru_b_x), rows(lru_lambda), rows(sg_norm), sg_w, sgb_full)
        a_out = _sb_attention(qkv)
        kv = _mem_kv(l, mem, rows(norm_mem), wkv_b)
        x3 = _mix_out_xattn(l, x, a_out, b_out, c_out, rows(mix_norm), w_out_b, rows(norm_mix_post),
                            rows(norm_xa_pre), wq_b, kv, wo_b, rows(norm_xa_post))
        x = _ffn(l, x3, rows(norm_ffn_pre), wup_b, ffn_conv_w, rows(ffn_conv_b), wdown_b,
                 rows(norm_ffn_post))
    return x
```

```python
import functools

import jax
import jax.numpy as jnp
from jax import lax
from jax.experimental import pallas as pl
from jax.experimental.pallas import tpu as pltpu

F32 = jnp.float32
BF16 = jnp.bfloat16

D_MODEL = 1024
CHUNK = 64
EPS = 1e-6
SB_HEAD_DIM = 64
SB_WIDTH = 512
LRU_WIDTH = 256
LRU_BLOCKS = 4
LRU_CONV = 4
LRU_C = 8.0
SG_WIDTH = 256
SG_GROUPS = 4
SG_GROUP_DIM = 64
SG_CHUNK = 128
IN_WIDTH = 2560
QKV_WIDTH = 3 * SB_WIDTH
XA_HEADS = 4
XA_HEAD_DIM = 256
D_FF = 2816
FFN_CONV = 3

SUBLANES = 8
LANES = 128
VMEM_LIMIT_BYTES = 56 * 1024 * 1024

ROW_TILE = 512
XA_TILE = 1024
FFN_TILE = 1024
SB_BLOCK = 128
SB_QUERY_BLOCKS = 4
FF_CHUNK = 256
FFN_Z_SLOTS = 4
SB_DEAD_LOG = -110.0
LOG2_E = 1.4426950408889634
SB_MASKED_SCORE = -1e30
SB_STATIC_BLOCKS = 3


def _params(*semantics):
    return pltpu.CompilerParams(dimension_semantics=semantics, vmem_limit_bytes=VMEM_LIMIT_BYTES)


def _layer_spec(l, shape):
    index = (l,) + (0,) * len(shape)
    return pl.BlockSpec((None,) + tuple(shape), lambda *_: index, pipeline_mode=pl.Buffered(1))


def _rms(x):
    return x * lax.rsqrt(jnp.mean(x * x, axis=-1, keepdims=True) + EPS)


def _gelu(x):
    return 0.5 * x * (1.0 + jnp.tanh(0.7978845608028654 * (x + 0.044715 * (x * x * x))))


def _sigmoid(x):
    return 1.0 / (1.0 + jnp.exp(-x))


def _in_mix_kernel(x_ref, g_ref, w_ref, cw_ref, cb_ref, wa_ref, ba_ref, wx_ref, bx_ref, lam_ref,
                   sgn_ref, sgw_ref, sgb_ref, qkv_ref, b_ref, c_ref, xpad_ref, h_ref):
    i = pl.program_id(1)
    tm = ROW_TILE
    w = LRU_WIDTH

    @pl.when(i == 0)
    def _():
        h_ref[...] = jnp.zeros_like(h_ref)
        xpad_ref[0:SUBLANES, :] = jnp.zeros((SUBLANES, w), F32)

    h = (_rms(x_ref[...]) * g_ref[...]).astype(BF16)

    def proj(col, width):
        return jnp.dot(h, w_ref[:, col:col + width], preferred_element_type=F32)

    lru_in = proj(QKV_WIDTH, 2 * w)
    sg_in = proj(QKV_WIDTH + 2 * w, 2 * SG_WIDTH)

    xr = lru_in[:, 0:w]
    xpad_ref[SUBLANES:SUBLANES + tm, :] = xr
    xc = cb_ref[...] + cw_ref[LRU_CONV - 1:LRU_CONV, :] * xr
    for k in range(LRU_CONV - 1):
        off = SUBLANES - (LRU_CONV - 1) + k
        xc = xc + cw_ref[k:k + 1, :] * xpad_ref[off:off + tm, :]
    xpad_ref[0:SUBLANES, :] = xr[tm - SUBLANES:tm, :]

    qkv_ref[:, 0:SB_WIDTH] = (proj(0, SB_WIDTH) * (SB_HEAD_DIM ** -0.5)).astype(BF16)

    xcb = xc.astype(BF16)
    r = _sigmoid(jnp.dot(xcb, wa_ref[...], preferred_element_type=F32) + ba_ref[...])
    gate_i = _sigmoid(jnp.dot(xcb, wx_ref[...], preferred_element_type=F32) + bx_ref[...])

    qkv_ref[:, SB_WIDTH:2 * SB_WIDTH] = proj(SB_WIDTH, SB_WIDTH).astype(BF16)
    qkv_ref[:, 2 * SB_WIDTH:3 * SB_WIDTH] = proj(2 * SB_WIDTH, SB_WIDTH).astype(BF16)

    lam = lam_ref[...]
    softplus_neg_lam = jnp.maximum(-lam, 0.0) + jnp.log1p(jnp.exp(-jnp.abs(lam)))
    log_a = (-LRU_C) * r * softplus_neg_lam
    a = jnp.exp(log_a)
    u = jnp.sqrt(-jnp.tanh(log_a) * (a * a + 1.0)) * (gate_i * xc)

    y_gate = _gelu(lru_in[:, w:2 * w])
    slab_row = lax.broadcasted_iota(jnp.int32, (SUBLANES, w), 0)
    h_prev = h_ref[0:1, :]
    for k in range(tm // SUBLANES):
        rows8 = slice(k * SUBLANES, (k + 1) * SUBLANES)
        a8, u8 = a[rows8], u[rows8]
        d = 1
        while d < SUBLANES:
            valid = slab_row >= d
            u8 = a8 * jnp.where(valid, pltpu.roll(u8, d, 0), 0.0) + u8
            a8 = a8 * jnp.where(valid, pltpu.roll(a8, d, 0), 1.0)
            d *= 2
        h8 = u8 + a8 * h_prev
        b_ref[rows8, :] = h8 * y_gate[rows8]
        h_prev = h8[SUBLANES - 1:SUBLANES, :]
    h_ref[...] = jnp.broadcast_to(h_prev, h_ref.shape)

    su = _gelu(sg_in[:, 0:SG_WIDTH])
    sv = _gelu(sg_in[:, SG_WIDTH:2 * SG_WIDTH])
    vn = (_rms(sv) * sgn_ref[...]).astype(BF16)
    pi = lax.broadcasted_iota(jnp.int32, (SG_CHUNK, SG_CHUNK), 0)
    pj = lax.broadcasted_iota(jnp.int32, (SG_CHUNK, SG_CHUNK), 1)
    chunk_causal = (pj // CHUNK) <= (pi // CHUNK)
    lane = lax.broadcasted_iota(jnp.int32, (SG_CHUNK, SG_WIDTH), 1)
    wm = jnp.concatenate([jnp.where(chunk_causal, sgw_ref[g], 0.0).astype(BF16) for g in range(SG_GROUPS)],
                         axis=0)
    for n in range(tm // SG_CHUNK):
        chunk = slice(n * SG_CHUNK, (n + 1) * SG_CHUNK)
        mg = jnp.dot(wm, vn[chunk, :], preferred_element_type=F32)
        mixed = mg[0:SG_CHUNK]
        for g in range(1, SG_GROUPS):
            mixed = jnp.where(lane >= g * SG_GROUP_DIM, mg[g * SG_CHUNK:(g + 1) * SG_CHUNK], mixed)
        c_ref[chunk, :] = su[chunk, :] * (mixed + sgb_ref[...])


def _in_mix(l, x3, g, w_bf16, cw, cb, wa_bd, ba, wx_bd, bx, lam, sgn, sgw, sgb_full):
    b, s, _ = x3.shape
    tm = ROW_TILE
    w = LRU_WIDTH
    rows = lambda width: pl.BlockSpec((None, tm, width), lambda bi, i: (bi, i, 0))
    return pl.pallas_call(
        _in_mix_kernel,
        grid=(b, s // tm),
        in_specs=[
            rows(D_MODEL),
            _layer_spec(l, (1, D_MODEL)),
            _layer_spec(l, (D_MODEL, IN_WIDTH)),
            _layer_spec(l, (LRU_CONV, w)),
            _layer_spec(l, (1, w)),
            _layer_spec(l, (w, w)),
            _layer_spec(l, (1, w)),
            _layer_spec(l, (w, w)),
            _layer_spec(l, (1, w)),
            _layer_spec(l, (1, w)),
            _layer_spec(l, (1, SG_WIDTH)),
            _layer_spec(l, (SG_GROUPS, SG_CHUNK, SG_CHUNK)),
            _layer_spec(l, (SG_CHUNK, SG_WIDTH)),
        ],
        out_specs=[rows(QKV_WIDTH), rows(w), rows(SG_WIDTH)],
        out_shape=[
            jax.ShapeDtypeStruct((b, s, QKV_WIDTH), BF16),
            jax.ShapeDtypeStruct((b, s, w), F32),
            jax.ShapeDtypeStruct((b, s, SG_WIDTH), F32),
        ],
        scratch_shapes=[
            pltpu.VMEM((tm + SUBLANES, w), F32),
            pltpu.VMEM((SUBLANES, w), F32),
        ],
        compiler_params=_params("parallel", "arbitrary"),
        name="in_mix",
    )(x3, g, w_bf16, cw, cb, wa_bd, ba, wx_bd, bx, lam, sgn, sgw, sgb_full)


def _sb_attn_kernel(q_ref, k_ref, v_ref, o_ref):
    blk = SB_BLOCK
    pairs = SB_WIDTH // LANES
    row = lax.broadcasted_iota(jnp.int32, (blk, 2 * blk), 0)
    col = lax.broadcasted_iota(jnp.int32, (blk, 2 * blk), 1)
    causal = (col % blk) < row
    trow = lax.broadcasted_iota(jnp.int32, (2 * blk, 2 * blk), 0)
    tcol = lax.broadcasted_iota(jnp.int32, (2 * blk, 2 * blk), 1)
    suffix_neg_ones = -jnp.logical_and(trow // blk == tcol // blk, trow >= tcol).astype(BF16)
    head0 = lax.broadcasted_iota(jnp.int32, (blk, LANES), 1) < SB_HEAD_DIM

    def split_heads(t):
        zero = jnp.zeros_like(t)
        return jnp.concatenate([jnp.where(head0, t, zero), jnp.where(head0, zero, t)], axis=0)

    def key_rows(j):
        return pl.ds(pl.multiple_of(j * blk, blk), blk)

    def scores(q_rows, p, j, allowed):
        cols = slice(p * LANES, (p + 1) * LANES)
        kk = split_heads(k_ref[key_rows(j), cols])
        s = lax.dot_general(q_ref[q_rows, cols], kk, (((1,), (1,)), ((), ())), preferred_element_type=F32)
        if allowed is not None:
            s = jnp.where(allowed, s, SB_MASKED_SCORE)
        return s

    def suffix_sum(s):
        neg_ls = jnp.maximum(s, 0.0) + jnp.log(1.0 + jnp.exp2(jnp.abs(s) * (-LOG2_E)))
        hi = neg_ls.astype(BF16)
        lo = (neg_ls - hi.astype(F32)).astype(BF16)
        return (jnp.dot(hi, suffix_neg_ones, preferred_element_type=F32)
                + jnp.dot(lo, suffix_neg_ones, preferred_element_type=F32))

    def weighted_values(p, j, s, incl, carry):
        cols = slice(p * LANES, (p + 1) * LANES)
        vv = split_heads(v_ref[key_rows(j), cols])
        c0, c1 = carry
        carry_b = jnp.concatenate([jnp.broadcast_to(c0, (blk, blk)), jnp.broadcast_to(c1, (blk, blk))], axis=1)
        w = jnp.exp2((s + incl + carry_b) * LOG2_E)
        return jnp.dot(w.astype(BF16), vv, preferred_element_type=F32)

    def key_blocks(q_rows, blocks, carries, accs):
        tiles = [(j, p, allowed) for j, allowed in blocks for p in range(pairs)]
        s_all = [scores(q_rows, p, j, allowed) for j, p, allowed in tiles]
        incl_all = [suffix_sum(s) for s in s_all]
        carries, accs = list(carries), list(accs)
        for (j, p, _), s, incl in zip(tiles, s_all, incl_all):
            accs[p] = accs[p] + weighted_values(p, j, s, incl, carries[p])
            carries[p] = (carries[p][0] + incl[:, 0:1], carries[p][1] + incl[:, blk:blk + 1])
        return tuple(carries), tuple(accs)

    def query_block(r, unused):
        i = pl.program_id(1) * SB_QUERY_BLOCKS + r
        q_rows = pl.ds(pl.multiple_of(r * blk, blk), blk)
        zero_col = jnp.zeros((blk, 1), F32)
        carries = tuple((zero_col, zero_col) for _ in range(pairs))
        accs = tuple(jnp.zeros((blk, LANES), F32) for _ in range(pairs))
        static_blocks = [(i, causal)] + [(jnp.maximum(i - d, 0), i - d >= 0)
                                         for d in range(1, SB_STATIC_BLOCKS)]
        carries, accs = key_blocks(q_rows, static_blocks, carries, accs)

        def cond(state):
            j, carries, _ = state
            live = functools.reduce(jnp.maximum, [c for pair in carries for c in pair])
            return jnp.logical_and(j >= 0, jnp.max(live) > SB_DEAD_LOG)

        def body(state):
            j, carries, accs = state
            carries, accs = key_blocks(q_rows, [(j, None)], carries, accs)
            return j - 1, carries, accs

        _, _, accs = lax.while_loop(cond, body, (i - SB_STATIC_BLOCKS, carries, accs))
        for p in range(pairs):
            o_ref[q_rows, p * LANES:(p + 1) * LANES] = accs[p]
        return unused

    lax.fori_loop(0, SB_QUERY_BLOCKS, query_block, 0)


def _sb_attention(qkv3):
    b, s, _ = qkv3.shape
    q_tile = SB_QUERY_BLOCKS * SB_BLOCK
    return pl.pallas_call(
        _sb_attn_kernel,
        grid=(b, s // q_tile),
        in_specs=[
            pl.BlockSpec((None, q_tile, SB_WIDTH), lambda bi, i: (bi, i, 0)),
            pl.BlockSpec((None, s, SB_WIDTH), lambda bi, i: (bi, 0, 1)),
            pl.BlockSpec((None, s, SB_WIDTH), lambda bi, i: (bi, 0, 2)),
        ],
        out_specs=pl.BlockSpec((None, q_tile, SB_WIDTH), lambda bi, i: (bi, i, 0)),
        out_shape=jax.ShapeDtypeStruct((b, s, SB_WIDTH), F32),
        compiler_params=_params("parallel", "parallel"),
        name="sb_attn",
    )(qkv3, qkv3, qkv3)


def _mem_kv_kernel(mem_ref, g_ref, w_ref, kv_ref):
    mn = (_rms(mem_ref[...]) * g_ref[...]).astype(BF16)
    kv_ref[...] = jnp.dot(mn, w_ref[...], preferred_element_type=F32).astype(BF16)


def _mem_kv(l, mem, g, w_bf16):
    b, m, _ = mem.shape
    return pl.pallas_call(
        _mem_kv_kernel,
        grid=(b,),
        in_specs=[
            pl.BlockSpec((None, m, D_MODEL), lambda bi: (bi, 0, 0)),
            _layer_spec(l, (1, D_MODEL)),
            _layer_spec(l, (D_MODEL, 2 * D_MODEL)),
        ],
        out_specs=pl.BlockSpec((None, m, 2 * D_MODEL), lambda bi: (bi, 0, 0)),
        out_shape=jax.ShapeDtypeStruct((b, m, 2 * D_MODEL), BF16),
        compiler_params=_params("parallel"),
        name="mem_kv",
    )(mem, g, w_bf16)


def _mix_out_xattn_kernel(x_ref, a_ref, b_ref, c_ref, mn_ref, wout_ref, gmix_ref, gpre_ref, wq_ref,
                          kv_ref, wo_ref, gpost_ref, o_ref):
    groups = [slice(g * ROW_TILE, (g + 1) * ROW_TILE) for g in range(XA_TILE // ROW_TILE)]
    heads = [slice(hd * XA_HEAD_DIM, (hd + 1) * XA_HEAD_DIM) for hd in range(XA_HEADS)]
    nt = (((1,), (1,)), ((), ()))

    mixed = [(jnp.concatenate([_rms(a_ref[r, :]), _rms(b_ref[r, :]), _rms(c_ref[r, :])], axis=-1)
              * mn_ref[...]).astype(BF16) for r in groups]
    y = [jnp.dot(m, wout_ref[...], preferred_element_type=F32) for m in mixed]
    x1 = [x_ref[r, :] + _rms(yg) * gmix_ref[...] for r, yg in zip(groups, y)]
    h = [(_rms(xg) * gpre_ref[...]).astype(BF16) for xg in x1]
    q = [(jnp.dot(hg, wq_ref[...], preferred_element_type=F32) * (XA_HEAD_DIM ** -0.5)).astype(BF16)
         for hg in h]
    s = [[lax.dot_general(qg[:, hs], kv_ref[:, hs], nt, preferred_element_type=F32) for hs in heads]
         for qg in q]
    e = [[jnp.exp(sh - jnp.max(sh, axis=-1, keepdims=True)) for sh in sg] for sg in s]
    p = [[(eh / jnp.sum(eh, axis=-1, keepdims=True)).astype(BF16) for eh in eg] for eg in e]
    o = [jnp.concatenate(
        [jnp.dot(ph, kv_ref[:, D_MODEL + hs.start:D_MODEL + hs.stop], preferred_element_type=F32).astype(BF16)
         for ph, hs in zip(pg, heads)], axis=-1) for pg in p]
    y2 = [jnp.dot(og, wo_ref[...], preferred_element_type=F32) for og in o]
    for r, xg, yg in zip(groups, x1, y2):
        o_ref[r, :] = xg + _rms(yg) * gpost_ref[...]


def _mix_out_xattn(l, x3, a3, b3, c3, mn, wout_bf16, gmix, gpre, wq_bf16, kv, wo_bf16, gpost):
    b, s, _ = x3.shape
    m = kv.shape[1]
    tm = XA_TILE
    rows = lambda width: pl.BlockSpec((None, tm, width), lambda bi, i: (bi, i, 0))
    return pl.pallas_call(
        _mix_out_xattn_kernel,
        grid=(b, s // tm),
        in_specs=[
            rows(D_MODEL), rows(SB_WIDTH), rows(LRU_WIDTH), rows(SG_WIDTH),
            _layer_spec(l, (1, D_MODEL)),
            _layer_spec(l, (D_MODEL, D_MODEL)),
            _layer_spec(l, (1, D_MODEL)),
            _layer_spec(l, (1, D_MODEL)),
            _layer_spec(l, (D_MODEL, D_MODEL)),
            pl.BlockSpec((None, m, 2 * D_MODEL), lambda bi, i: (bi, 0, 0)),
            _layer_spec(l, (D_MODEL, D_MODEL)),
            _layer_spec(l, (1, D_MODEL)),
        ],
        out_specs=rows(D_MODEL),
        out_shape=jax.ShapeDtypeStruct((b, s, D_MODEL), F32),
        compiler_params=_params("parallel", "parallel"),
        name="mix_out_xattn",
    )(x3, a3, b3, c3, mn, wout_bf16, gmix, gpre, wq_bf16, kv, wo_bf16, gpost)


def _ffn_kernel(x_ref, halo_ref, gpre_ref, wup_ref, cw_ref, cb_ref, wdown_ref, gpost_ref, o_ref,
                z_ref, act_ref):
    i = pl.program_id(1)
    tm = ROW_TILE
    n_groups = FFN_TILE // tm

    def up_proj(h, col):
        return jnp.dot(h, wup_ref[:, col:col + FF_CHUNK], preferred_element_type=F32)

    def conv(g, z, col, slot):
        halves = []
        for hf in range(FF_CHUNK // LANES):
            lanes = slice(hf * LANES, (hf + 1) * LANES)
            wcol = slice(col + hf * LANES, col + (hf + 1) * LANES)
            z_halo = jnp.where(i == 0, 0.0, z[0:SUBLANES, lanes]) if g == 0 else z[0:SUBLANES, lanes]
            z_ref[g, slot, hf, 0:SUBLANES, :] = z_halo
            z_ref[g, slot, hf, SUBLANES:, :] = z[SUBLANES:, lanes]
            out = cb_ref[:, wcol] + cw_ref[FFN_CONV - 1:FFN_CONV, wcol] * z[SUBLANES:, lanes]
            for k in range(FFN_CONV - 1):
                off = SUBLANES - (FFN_CONV - 1) + k
                out = out + cw_ref[k:k + 1, wcol] * z_ref[g, slot, hf, off:off + tm, :]
            halves.append(out)
        return jnp.concatenate(halves, axis=-1)

    xs = [x_ref[g * tm:(g + 1) * tm, :] for g in range(n_groups)]
    halos = [halo_ref[...]] + [x_ref[g * tm - SUBLANES:g * tm, :] for g in range(1, n_groups)]
    hs = [(_rms(jnp.concatenate([halo, x], axis=0)) * gpre_ref[...]).astype(BF16)
          for halo, x in zip(halos, xs)]
    for g in range(n_groups):
        for c in range(D_FF // FF_CHUNK):
            gate = conv(g, up_proj(hs[g], c * FF_CHUNK), c * FF_CHUNK, (2 * c) % FFN_Z_SLOTS)
            up = conv(g, up_proj(hs[g], D_FF + c * FF_CHUNK), D_FF + c * FF_CHUNK, (2 * c + 1) % FFN_Z_SLOTS)
            act_ref[g, :, c * FF_CHUNK:(c + 1) * FF_CHUNK] = (_gelu(gate) * up).astype(BF16)
    ys = [jnp.dot(act_ref[g], wdown_ref[...], preferred_element_type=F32) for g in range(n_groups)]
    for g in range(n_groups):
        o_ref[g * tm:(g + 1) * tm, :] = xs[g] + _rms(ys[g]) * gpost_ref[...]


def _ffn(l, x3, gpre, wup_bf16, cw, cb, wdown_bf16, gpost):
    b, s, _ = x3.shape
    tm = FFN_TILE
    groups = FFN_TILE // ROW_TILE
    halo_blocks = tm // SUBLANES
    return pl.pallas_call(
        _ffn_kernel,
        grid=(b, s // tm),
        in_specs=[
            pl.BlockSpec((None, tm, D_MODEL), lambda bi, i: (bi, i, 0)),
            pl.BlockSpec((None, SUBLANES, D_MODEL),
                         lambda bi, i: (bi, jnp.maximum(i * halo_blocks - 1, 0), 0)),
            _layer_spec(l, (1, D_MODEL)),
            _layer_spec(l, (D_MODEL, 2 * D_FF)),
            _layer_spec(l, (FFN_CONV, 2 * D_FF)),
            _layer_spec(l, (1, 2 * D_FF)),
            _layer_spec(l, (D_FF, D_MODEL)),
            _layer_spec(l, (1, D_MODEL)),
        ],
        out_specs=pl.BlockSpec((None, tm, D_MODEL), lambda bi, i: (bi, i, 0)),
        out_shape=jax.ShapeDtypeStruct((b, s, D_MODEL), F32),
        scratch_shapes=[
            pltpu.VMEM((groups, FFN_Z_SLOTS, FF_CHUNK // LANES, ROW_TILE + SUBLANES, LANES), F32),
            pltpu.VMEM((groups, ROW_TILE, D_FF), BF16)],
        compiler_params=_params("parallel", "parallel"),
        name="ffn",
    )(x3, x3, gpre, wup_bf16, cw, cb, wdown_bf16, gpost)


def _block_diag(w):
    layers, n, c, d = w.shape
    eye = jnp.eye(n, dtype=w.dtype)
    return (eye[None, :, None, :, None] * w[:, :, :, None, :]).reshape(layers, n * c, n * d)


def kernel(x, mem, norm_mix_pre, w_in, lru_conv_w, lru_conv_b, lru_w_a, lru_b_a, lru_w_x, lru_b_x, lru_lambda, sg_norm, sg_w, sg_b, mix_norm, w_out, norm_mix_post, norm_xa_pre, norm_mem, xa_w_q, xa_w_kv, xa_w_o, norm_xa_post, norm_ffn_pre, ffn_w_up, ffn_conv_w, ffn_conv_b, ffn_w_down, norm_ffn_post):
    depth = w_in.shape[0]
    rows = lambda v: v.reshape(depth, 1, -1)
    w_in_b, w_out_b = w_in.astype(BF16), w_out.astype(BF16)
    wq_b, wkv_b, wo_b = xa_w_q.astype(BF16), xa_w_kv.astype(BF16), xa_w_o.astype(BF16)
    wup_b, wdown_b = ffn_w_up.astype(BF16), ffn_w_down.astype(BF16)
    wa_bd, wx_bd = _block_diag(lru_w_a).astype(BF16), _block_diag(lru_w_x).astype(BF16)
    sgb_full = jnp.repeat(jnp.swapaxes(sg_b, 1, 2), SG_GROUP_DIM, axis=2)

    for l in range(depth):
        qkv, b_out, c_out = _in_mix(
            l, x, rows(norm_mix_pre), w_in_b, lru_conv_w, rows(lru_conv_b), wa_bd, rows(lru_b_a),
            wx_bd, rows(lru_b_x), rows(lru_lambda), rows(sg_norm), sg_w, sgb_full)
        a_out = _sb_attention(qkv)
        kv = _mem_kv(l, mem, rows(norm_mem), wkv_b)
        x3 = _mix_out_xattn(l, x, a_out, b_out, c_out, rows(mix_norm), w_out_b, rows(norm_mix_post),
                            rows(norm_xa_pre), wq_b, kv, wo_b, rows(norm_xa_post))
        x = _ffn(l, x3, rows(norm_ffn_pre), wup_b, ffn_conv_w, rows(ffn_conv_b), wdown_b,
                 rows(norm_ffn_post))
    return x
```
